```python
import jax, jax.numpy as jnp
from jax import lax
import numpy as np

D_MODEL = 4096
BATCH = 4
SEQ = 2048
DEPTH = 4
DEC_BATCH = 128
DEC_SEQ = 8
PAST_LEN = 16384
PAGE_SIZE = 128

MIX_WIDTH = D_MODEL
GROUP_WIDTH = MIX_WIDTH // 4
GDN_WIDTH = GROUP_WIDTH
GDN_HEAD_DIM = 128
GDN_HEADS = GDN_WIDTH // GDN_HEAD_DIM
GDN_CHUNK = 64
SHORT_CONV = 4
SSM_WIDTH = GROUP_WIDTH
SSM_HEAD_DIM = 64
SSM_HEADS = SSM_WIDTH // SSM_HEAD_DIM
SSM_GROUPS = 2
SSM_STATE = 128
SSM_CONV_DIM = SSM_WIDTH + 2 * SSM_GROUPS * SSM_STATE
SSM_CHUNK = 64
CONF_WIDTH = GROUP_WIDTH
CONF_KERNEL = 31
POOL_WIDTH = MIX_WIDTH - GDN_WIDTH - SSM_WIDTH - CONF_WIDTH
POOL_WINDOWS = (2, 4, 8, 16)
POOL_GROUP = POOL_WIDTH // len(POOL_WINDOWS)
POOL_MAX = max(POOL_WINDOWS)
POOL_HIST = POOL_MAX - 1
D_FF = ((8 * D_MODEL + 3 * 256 - 1) // (3 * 256)) * 256
IN_SIZES = (3 * GDN_WIDTH, GDN_WIDTH, GDN_HEADS, GDN_HEADS,
            SSM_WIDTH, SSM_CONV_DIM, SSM_HEADS,
            2 * CONF_WIDTH,
            POOL_WIDTH)
IN_COLS = sum(IN_SIZES)
EPS = 1e-6

kernel_name = "hybrid_gdn_ssd_conformer_pool_decoder_step"


def rmsnorm(x, w):
    xf = x.astype(jnp.float32)
    y = xf * lax.rsqrt(jnp.mean(xf * xf, axis=-1, keepdims=True) + EPS)
    return y * w.astype(jnp.float32)


def layernorm(x, w, b):
    xf = x.astype(jnp.float32)
    mu = jnp.mean(xf, axis=-1, keepdims=True)
    xc = xf - mu
    y = xc * lax.rsqrt(jnp.mean(xc * xc, axis=-1, keepdims=True) + EPS)
    return y * w.astype(jnp.float32) + b.astype(jnp.float32)


def l2norm(x):
    return x * lax.rsqrt(jnp.sum(x * x, axis=-1, keepdims=True) + EPS)


def split_cols(t, sizes):
    parts, start = [], 0
    for s in sizes:
        parts.append(t[..., start:start + s])
        start += s
    return parts


def pad_seq(t, pad):
    return jnp.pad(t, [(0, 0), (0, pad)] + [(0, 0)] * (t.ndim - 2))


def causal_dwconv(hist, u, w):
    k = w.shape[0]
    ext = jnp.concatenate([hist.astype(u.dtype), u], axis=1)
    y = lax.conv_general_dilated(ext, w.astype(u.dtype)[:, None, :], (1,), 'VALID',
                                 dimension_numbers=('NWC', 'WIO', 'NWC'),
                                 feature_group_count=u.shape[-1])
    return y, ext[:, ext.shape[1] - (k - 1):]


def gated_delta_chunked(q, k, v, g, beta, s0):
    bsz, L, H, DK = q.shape
    DV = v.shape[-1]
    C = GDN_CHUNK
    Lp = -(-L // C) * C
    pad = Lp - L
    q, k, v, g, beta = (pad_seq(t, pad) for t in (q, k, v, g, beta))
    n = Lp // C
    chunk4 = lambda t: t.reshape(bsz, n, C, H, t.shape[-1]).transpose(0, 3, 1, 2, 4)
    chunk3 = lambda t: t.reshape(bsz, n, C, H).transpose(0, 3, 1, 2)
    q, k, v = chunk4(q), chunk4(k), chunk4(v)
    g, beta = chunk3(g), chunk3(beta)
    gc = jnp.cumsum(g, axis=-1)
    tril = jnp.tril(jnp.ones((C, C), bool))
    strict = jnp.tril(jnp.ones((C, C), bool), -1)
    diff = gc[..., :, None] - gc[..., None, :]
    decay = jnp.where(tril, jnp.exp(jnp.where(tril, diff, 0.0)), 0.0)
    kb = k * beta[..., None]
    vb = v * beta[..., None]
    lmat = jnp.where(strict, jnp.einsum('bhncd,bhnsd->bhncs', kb, k) * decay, 0.0)
    a = jnp.eye(C, dtype=lmat.dtype) + lmat
    rhs = jnp.concatenate([vb, kb * jnp.exp(gc)[..., None]], axis=-1)
    sol = lax.linalg.triangular_solve(a, rhs, left_side=True, lower=True, unit_diagonal=True)
    u, w = sol[..., :DV], sol[..., DV:]
    attn = jnp.where(tril, jnp.einsum('bhncd,bhnsd->bhncs', q, k) * decay, 0.0)

    def step(S, inp):
        qn, kn, un, wn, gn, an = inp
        v_new = un - jnp.einsum('bhcd,bhde->bhce', wn, S)
        o = (jnp.einsum('bhcd,bhde->bhce', qn * jnp.exp(gn)[..., None], S)
             + jnp.einsum('bhcs,bhse->bhce', an, v_new))
        glast = gn[..., -1:]
        S = (S * jnp.exp(glast)[..., None]
             + jnp.einsum('bhcd,bhce->bhde', kn * jnp.exp(glast - gn)[..., None], v_new))
        return S, o

    xs = tuple(jnp.moveaxis(t, 2, 0) for t in (q, k, u, w, gc, attn))
    s_last, o = lax.scan(step, s0, xs)
    o = o.transpose(1, 0, 3, 2, 4).reshape(bsz, Lp, H, DV)[:, :L]
    return o, s_last


def ssd_chunked(x, dt, A, Bm, Cm, h0):
    bsz, L, H, P = x.shape
    G, N = Bm.shape[2], Bm.shape[3]
    R = H // G
    Q = SSM_CHUNK
    Lp = -(-L // Q) * Q
    pad = Lp - L
    x, dt, Bm, Cm = (pad_seq(t, pad) for t in (x, dt, Bm, Cm))
    nc = Lp // Q
    x = x.reshape(bsz, nc, Q, G, R, P)
    dt = dt.reshape(bsz, nc, Q, G, R)
    Bm = Bm.reshape(bsz, nc, Q, G, N)
    Cm = Cm.reshape(bsz, nc, Q, G, N)
    acum = jnp.cumsum(dt * A.reshape(G, R), axis=2)
    causal = jnp.tril(jnp.ones((Q, Q), bool))[:, :, None, None]
    seg = acum[:, :, :, None] - acum[:, :, None, :]
    lmat = jnp.where(causal, jnp.exp(jnp.where(causal, seg, 0.0)), 0.0)
    cb = jnp.einsum('bcqgn,bcsgn->bcqsg', Cm, Bm)
    m = cb[..., None] * lmat * dt[:, :, None]
    y_diag = jnp.einsum('bcqsgr,bcsgrp->bcqgrp', m, x)
    decay_s = jnp.exp(acum[:, :, -1:] - acum) * dt
    chunk_states = jnp.einsum('bcsgn,bcsgr,bcsgrp->bcgrpn', Bm, decay_s, x)
    chunk_decay = jnp.exp(acum[:, :, -1])

    def step(h, inp):
        dec, st = inp
        return h * dec[..., None, None] + st, h

    h_last, h_prev = lax.scan(step, h0.reshape(bsz, G, R, P, N),
                              (jnp.moveaxis(chunk_decay, 1, 0), jnp.moveaxis(chunk_states, 1, 0)))
    h_prev = jnp.moveaxis(h_prev, 0, 1)
    y_off = jnp.einsum('bcqgn,bcgrpn->bcqgrp', Cm, h_prev) * jnp.exp(acum)[..., None]
    y = (y_diag + y_off).reshape(bsz, Lp, H, P)[:, :L]
    return y, h_last.reshape(bsz, H, P, N)


def gdn_mixer(p_qkv, p_z, p_b, p_a, hist, s0, conv_w, a_log, dt_bias, norm_w):
    bsz, L, _ = p_qkv.shape
    qkv, new_hist = causal_dwconv(hist, p_qkv, conv_w)
    qkv = jax.nn.silu(qkv.astype(jnp.float32)).reshape(bsz, L, 3, GDN_HEADS, GDN_HEAD_DIM)
    q = l2norm(qkv[:, :, 0]) * (GDN_HEAD_DIM ** -0.5)
    k = l2norm(qkv[:, :, 1])
    v = qkv[:, :, 2]
    beta = jax.nn.sigmoid(p_b.astype(jnp.float32))
    g = -jnp.exp(a_log.astype(jnp.float32)) * jax.nn.softplus(p_a.astype(jnp.float32) + dt_bias.astype(jnp.float32))
    o, s_new = gated_delta_chunked(q, k, v, g, beta, s0.astype(jnp.float32))
    z = p_z.astype(jnp.float32).reshape(bsz, L, GDN_HEADS, GDN_HEAD_DIM)
    o = rmsnorm(o, norm_w) * jax.nn.silu(z)
    return o.reshape(bsz, L, GDN_WIDTH), new_hist, s_new


def ssm_mixer(p_z, p_xbc, p_dt, hist, h0, conv_w, conv_b, a_log, dt_bias, d_skip, norm_w):
    bsz, L, _ = p_xbc.shape
    xbc, new_hist = causal_dwconv(hist, p_xbc, conv_w)
    xbc = jax.nn.silu(xbc.astype(jnp.float32) + conv_b.astype(jnp.float32))
    gn = SSM_GROUPS * SSM_STATE
    x = xbc[..., :SSM_WIDTH].reshape(bsz, L, SSM_HEADS, SSM_HEAD_DIM)
    Bm = xbc[..., SSM_WIDTH:SSM_WIDTH + gn].reshape(bsz, L, SSM_GROUPS, SSM_STATE)
    Cm = xbc[..., SSM_WIDTH + gn:].reshape(bsz, L, SSM_GROUPS, SSM_STATE)
    dt = jax.nn.softplus(p_dt.astype(jnp.float32) + dt_bias.astype(jnp.float32))
    A = -jnp.exp(a_log.astype(jnp.float32))
    y, h_new = ssd_chunked(x, dt, A, Bm, Cm, h0.astype(jnp.float32))
    y = y + d_skip.astype(jnp.float32)[:, None] * x
    y = y.reshape(bsz, L, SSM_WIDTH) * jax.nn.silu(p_z.astype(jnp.float32))
    y = rmsnorm(y.reshape(bsz, L, SSM_GROUPS, SSM_WIDTH // SSM_GROUPS),
                norm_w.reshape(SSM_GROUPS, SSM_WIDTH // SSM_GROUPS))
    return y.reshape(bsz, L, SSM_WIDTH), new_hist, h_new


def conformer_mixer(p_glu, hist, dw_w, dw_b, ln_w, ln_b):
    a, gate = p_glu[..., :CONF_WIDTH], p_glu[..., CONF_WIDTH:]
    u = a * jax.nn.sigmoid(gate)
    c, new_hist = causal_dwconv(hist, u, dw_w)
    c = layernorm(c.astype(jnp.float32) + dw_b.astype(jnp.float32), ln_w, ln_b)
    return jax.nn.silu(c), new_hist


def pool_mixer(u, hist, pos0, pool_w, pool_scale):
    bsz, L, _ = u.shape
    ext = jnp.concatenate([hist.astype(u.dtype), u], axis=1).astype(jnp.float32)
    csum = jnp.concatenate([jnp.zeros((bsz, 1, POOL_WIDTH), jnp.float32),
                            lax.cumsum(ext, axis=1)], axis=1)
    pos = (pos0 + jnp.arange(L)).astype(jnp.float32)
    uf = u.astype(jnp.float32)
    outs = []
    for gi, w in enumerate(POOL_WINDOWS):
        lo_c, hi_c = gi * POOL_GROUP, (gi + 1) * POOL_GROUP
        hi = csum[:, POOL_MAX:POOL_MAX + L, lo_c:hi_c]
        lo = csum[:, POOL_MAX - w:POOL_MAX - w + L, lo_c:hi_c]
        cnt = jnp.minimum(jnp.float32(w), pos + 1.0)
        outs.append((hi - lo) / cnt[None, :, None] - uf[..., lo_c:hi_c])
    pooled = jnp.stack(outs, axis=2)
    y = jnp.einsum('blgc,gcd->blgd', pooled, pool_w.astype(jnp.float32)).reshape(bsz, L, POOL_WIDTH)
    return y * pool_scale.astype(jnp.float32), ext[:, ext.shape[1] - POOL_HIST:].astype(u.dtype)


def layer(x, pos0, st, lw):
    st_gdn, st_gdn_conv, st_ssm, st_ssm_conv, st_conv, st_pool = st
    h = rmsnorm(x, lw['norm_mix']).astype(x.dtype)
    proj = jnp.einsum('bld,de->ble', h, lw['w_in'])
    p_qkv, p_gz, p_gb, p_ga, p_sz, p_xbc, p_dt, p_glu, p_pool = split_cols(proj, IN_SIZES)
    o_gdn, n_gdn_conv, n_gdn = gdn_mixer(p_qkv, p_gz, p_gb, p_ga, st_gdn_conv, st_gdn,
                                         lw['gdn_conv_w'], lw['gdn_a_log'], lw['gdn_dt_bias'], lw['gdn_norm_w'])
    o_ssm, n_ssm_conv, n_ssm = ssm_mixer(p_sz, p_xbc, p_dt, st_ssm_conv, st_ssm,
                                         lw['ssm_conv_w'], lw['ssm_conv_b'], lw['ssm_a_log'],
                                         lw['ssm_dt_bias'], lw['ssm_d'], lw['ssm_norm_w'])
    o_conv, n_conv = conformer_mixer(p_glu, st_conv, lw['conf_dw_w'], lw['conf_dw_b'],
                                     lw['conf_ln_w'], lw['conf_ln_b'])
    o_pool, n_pool = pool_mixer(p_pool, st_pool, pos0, lw['pool_w'], lw['pool_scale'])
    mix = jnp.concatenate([o_gdn, o_ssm, o_conv, o_pool], axis=-1).astype(x.dtype)
    x = x + jnp.einsum('ble,ed->bld', mix, lw['w_out'])
    h = rmsnorm(x, lw['norm_ffn']).astype(x.dtype)
    ff = jax.nn.silu(jnp.einsum('bld,df->blf', h, lw['w_gate'])) * jnp.einsum('bld,df->blf', h, lw['w_up'])
    x = x + jnp.einsum('blf,fd->bld', ff, lw['w_down'])
    return x, (n_gdn, n_gdn_conv, n_ssm, n_ssm_conv, n_conv, n_pool)


def trunk(x, pos0, states, params, norm_final):
    collected = [[] for _ in range(len(states))]
    for l in range(DEPTH):
        lw = {name: arr[l] for name, arr in params.items()}
        st = tuple(s[l] for s in states)
        x, new_st = layer(x, pos0, st, lw)
        for i, s in enumerate(new_st):
            collected[i].append(s)
    y = rmsnorm(x, norm_final).astype(x.dtype)
    return y, tuple(jnp.stack(c, axis=0) for c in collected)


def setup_inputs(seed: int = 0) -> dict:
    key = jax.random.key(seed)
    ks = iter(jax.random.split(key, 48))
    nrm = lambda shape, s: jax.random.normal(next(ks), shape, jnp.float32) * s
    uni = lambda shape, lo, hi: jax.random.uniform(next(ks), shape, jnp.float32, lo, hi)

    def dt_bias_init(shape):
        dt = jnp.exp(uni(shape, float(np.log(1e-3)), float(np.log(1e-1))))
        return dt + jnp.log(-jnp.expm1(-dt))

    return {
        'x_prompt': nrm((BATCH, SEQ, D_MODEL), 1.0),
        'x_sample': nrm((DEC_BATCH, DEC_SEQ, D_MODEL), 1.0),
        'state_gdn': nrm((DEPTH, DEC_BATCH, GDN_HEADS, GDN_HEAD_DIM, GDN_HEAD_DIM), 0.1),
        'state_gdn_conv': nrm((DEPTH, DEC_BATCH, SHORT_CONV - 1, 3 * GDN_WIDTH), 1.0),
        'state_ssm': nrm((DEPTH, DEC_BATCH, SSM_HEADS, SSM_HEAD_DIM, SSM_STATE), 0.1),
        'state_ssm_conv': nrm((DEPTH, DEC_BATCH, SHORT_CONV - 1, SSM_CONV_DIM), 1.0),
        'state_conv': nrm((DEPTH, DEC_BATCH, CONF_KERNEL - 1, CONF_WIDTH), 0.5),
        'state_pool': nrm((DEPTH, DEC_BATCH, POOL_HIST, POOL_WIDTH), 1.0),
        'norm_mix': 1.0 + nrm((DEPTH, D_MODEL), 0.02),
        'w_in': nrm((DEPTH, D_MODEL, IN_COLS), D_MODEL ** -0.5),
        'gdn_conv_w': nrm((DEPTH, SHORT_CONV, 3 * GDN_WIDTH), SHORT_CONV ** -0.5),
        'gdn_a_log': jnp.log(uni((DEPTH, GDN_HEADS), 1.0, 16.0)),
        'gdn_dt_bias': dt_bias_init((DEPTH, GDN_HEADS)),
        'gdn_norm_w': 1.0 + nrm((DEPTH, GDN_HEAD_DIM), 0.02),
        'ssm_conv_w': nrm((DEPTH, SHORT_CONV, SSM_CONV_DIM), SHORT_CONV ** -0.5),
        'ssm_conv_b': nrm((DEPTH, SSM_CONV_DIM), 0.02),
        'ssm_a_log': jnp.log(uni((DEPTH, SSM_HEADS), 1.0, 16.0)),
        'ssm_dt_bias': dt_bias_init((DEPTH, SSM_HEADS)),
        'ssm_d': 1.0 + nrm((DEPTH, SSM_HEADS), 0.02),
        'ssm_norm_w': 1.0 + nrm((DEPTH, SSM_WIDTH), 0.02),
        'conf_dw_w': nrm((DEPTH, CONF_KERNEL, CONF_WIDTH), CONF_KERNEL ** -0.5),
        'conf_dw_b': nrm((DEPTH, CONF_WIDTH), 0.02),
        'conf_ln_w': 1.0 + nrm((DEPTH, CONF_WIDTH), 0.02),
        'conf_ln_b': nrm((DEPTH, CONF_WIDTH), 0.02),
        'pool_w': nrm((DEPTH, len(POOL_WINDOWS), POOL_GROUP, POOL_GROUP), POOL_GROUP ** -0.5),
        'pool_scale': 1.0 + nrm((DEPTH, POOL_WIDTH), 0.02),
        'w_out': nrm((DEPTH, MIX_WIDTH, D_MODEL), MIX_WIDTH ** -0.5),
        'norm_ffn': 1.0 + nrm((DEPTH, D_MODEL), 0.02),
        'w_gate': nrm((DEPTH, D_MODEL, D_FF), D_MODEL ** -0.5),
        'w_up': nrm((DEPTH, D_MODEL, D_FF), D_MODEL ** -0.5),
        'w_down': nrm((DEPTH, D_FF, D_MODEL), D_FF ** -0.5),
        'norm_final': 1.0 + nrm((D_MODEL,), 0.02),
    }


def reference(x_prompt, x_sample, state_gdn, state_gdn_conv, state_ssm, state_ssm_conv, state_conv, state_pool,
              norm_mix, w_in, gdn_conv_w, gdn_a_log, gdn_dt_bias, gdn_norm_w,
              ssm_conv_w, ssm_conv_b, ssm_a_log, ssm_dt_bias, ssm_d, ssm_norm_w,
              conf_dw_w, conf_dw_b, conf_ln_w, conf_ln_b, pool_w, pool_scale,
              w_out, norm_ffn, w_gate, w_up, w_down, norm_final):
    params = dict(norm_mix=norm_mix, w_in=w_in, gdn_conv_w=gdn_conv_w, gdn_a_log=gdn_a_log,
                  gdn_dt_bias=gdn_dt_bias, gdn_norm_w=gdn_norm_w, ssm_conv_w=ssm_conv_w, ssm_conv_b=ssm_conv_b,
                  ssm_a_log=ssm_a_log, ssm_dt_bias=ssm_dt_bias, ssm_d=ssm_d, ssm_norm_w=ssm_norm_w,
                  conf_dw_w=conf_dw_w, conf_dw_b=conf_dw_b, conf_ln_w=conf_ln_w, conf_ln_b=conf_ln_b,
                  pool_w=pool_w, pool_scale=pool_scale, w_out=w_out, norm_ffn=norm_ffn,
                  w_gate=w_gate, w_up=w_up, w_down=w_down)
    sample_states = (state_gdn, state_gdn_conv, state_ssm, state_ssm_conv, state_conv, state_pool)
    nb = x_prompt.shape[0]
    prompt_states = tuple(jnp.zeros((DEPTH, nb) + s.shape[2:], s.dtype) for s in sample_states)
    y_prompt, (p_gdn, p_gdn_conv, p_ssm, p_ssm_conv, p_conv, p_pool) = trunk(
        x_prompt, 0, prompt_states, params, norm_final)
    y_sample, (s_gdn, s_gdn_conv, s_ssm, s_ssm_conv, s_conv, s_pool) = trunk(
        x_sample, PAST_LEN, sample_states, params, norm_final)
    return (y_prompt, y_sample,
            p_gdn, p_gdn_conv, p_ssm, p_ssm_conv, p_conv, p_pool,
            s_gdn, s_gdn_conv, s_ssm, s_ssm_conv, s_conv, s_pool)
```

```python
import functools

import jax
import jax.numpy as jnp
from jax import lax
from jax.experimental import pallas as pl
from jax.experimental.pallas import tpu as pltpu

F32 = jnp.float32
BF16 = jnp.bfloat16

D_MODEL = 4096
DEPTH = 4
PAST_LEN = 16384
GROUP_WIDTH = 1024
GDN_HEADS = 8
GDN_HEAD_DIM = 128
SHORT_CONV = 4
SSM_HEADS = 16
SSM_HEAD_DIM = 64
SSM_GROUPS = 2
SSM_STATE = 128
SSM_CONV_DIM = 1536
CONF_KERNEL = 31
POOL_WINDOWS = (2, 4, 8, 16)
POOL_GROUP = 256
POOL_HIST = 15
D_FF = 11008
EPS = 1e-6

SUBLANES = 8
LANES = 128
VMEM_LIMIT = 56 * 1024 * 1024

COL_QKV = 0
COL_GZ = 3072
COL_SZ = 4096
COL_POOL = 5120
COL_GLU_A = 6144
COL_GLU_G = 7168
COL_X = 8192
COL_BC = 9216
COL_SMALL = 9728
LANE_GB = 0
LANE_GA = 8
LANE_DT = 16
IN_COLS_PAD = 9984
D_FF_PAD = 11264

SRC_QKV, SRC_GZ, SRC_GB, SRC_GA, SRC_SZ, SRC_XBC, SRC_DT, SRC_GLU, SRC_POOL = (
    0, 3072, 4096, 4104, 4112, 5136, 6672, 6688, 8736)

GDN_CARRY = 8
CONF_CARRY = 32
POOL_CARRY = 16


def _dot(a, b):
    return jnp.dot(a, b, preferred_element_type=F32)


def _dot_nt(a, b):
    return lax.dot_general(a, b, (((1,), (1,)), ((), ())), preferred_element_type=F32)


def _dot_tn(a, b):
    return lax.dot_general(a, b, (((0,), (0,)), ((), ())), preferred_element_type=F32)


def _split2(x):
    hi = x.astype(BF16)
    lo = (x - hi.astype(F32)).astype(BF16)
    return hi, lo


def _dot3(a, b):
    ah, al = _split2(a)
    bh, bl = _split2(b)
    return _dot(ah, bh) + (_dot(ah, bl) + _dot(al, bh))


def _cumsum_rows(x, tril):
    hi = x.astype(BF16)
    r = x - hi.astype(F32)
    mid = r.astype(BF16)
    lo = (r - mid.astype(F32)).astype(BF16)
    return _dot(tril, hi) + (_dot(tril, mid) + _dot(tril, lo))


def _silu(x):
    return x * jax.nn.sigmoid(x)


def _softplus(x):
    return jnp.maximum(x, 0.0) + jnp.log1p(jnp.exp(-jnp.abs(x)))


def _unit_lower_inverse(lm, n):
    rows = lax.broadcasted_iota(jnp.int32, (n, n), 0)
    cols = lax.broadcasted_iota(jnp.int32, (n, n), 1)
    eye = jnp.where(rows == cols, 1.0, 0.0).astype(F32)
    x = -lm
    t = eye + x
    p = x
    steps = max(n.bit_length() - 2, 0)
    for _ in range(steps):
        p = _dot3(p, p)
        t = t + _dot3(t, p)
    return t


def _rmsnorm_kernel(x_ref, w_ref, o_ref):
    x = x_ref[...]
    ms = jnp.mean(x * x, axis=-1, keepdims=True)
    o_ref[...] = (x * lax.rsqrt(ms + EPS) * w_ref[...]).astype(o_ref.dtype)


def _rmsnorm(x, w, out_dtype, *, row0=0, rows=None, tm=256):
    t, d = x.shape
    rows = t if rows is None else rows
    blk0 = row0 // tm
    return pl.pallas_call(
        _rmsnorm_kernel,
        grid=(rows // tm,),
        in_specs=[pl.BlockSpec((tm, d), lambda i: (blk0 + i, 0)),
                  pl.BlockSpec((1, d), lambda i: (0, 0))],
        out_specs=pl.BlockSpec((tm, d), lambda i: (i, 0)),
        out_shape=jax.ShapeDtypeStruct((rows, d), out_dtype),
        compiler_params=pltpu.CompilerParams(dimension_semantics=("parallel",),
                                             vmem_limit_bytes=VMEM_LIMIT),
        name="rmsnorm",
    )(x, w.reshape(1, d))


def _matmul_kernel(a_ref, b_ref, *rest, nk, has_res):
    if has_res:
        r_ref, o_ref = rest[0], rest[1]
        scratch = rest[2:]
    else:
        r_ref, o_ref = None, rest[0]
        scratch = rest[1:]
    part = _dot(a_ref[...], b_ref[...])
    if nk == 1:
        o_ref[...] = part + r_ref[...] if has_res else part
        return
    acc_ref = scratch[0]
    k = pl.program_id(2)

    @pl.when(k == 0)
    def _():
        acc_ref[...] = part

    @pl.when(k > 0)
    def _():
        acc_ref[...] += part

    @pl.when(k == nk - 1)
    def _():
        o_ref[...] = acc_ref[...] + r_ref[...] if has_res else acc_ref[...]


def _matmul(a, b, res=None, *, tm, tn, tk, name):
    m, kdim = a.shape
    n = b.shape[1]
    nk = kdim // tk
    in_specs = [pl.BlockSpec((tm, tk), lambda i, j, k: (i, k)),
                pl.BlockSpec((tk, tn), lambda i, j, k: (k, j))]
    args = [a, b]
    if res is not None:
        in_specs.append(pl.BlockSpec((tm, tn), lambda i, j, k: (i, j)))
        args.append(res)
    scratch = [pltpu.VMEM((tm, tn), F32)] if nk > 1 else []
    return pl.pallas_call(
        functools.partial(_matmul_kernel, nk=nk, has_res=res is not None),
        grid=(m // tm, n // tn, nk),
        in_specs=in_specs,
        out_specs=pl.BlockSpec((tm, tn), lambda i, j, k: (i, j)),
        out_shape=jax.ShapeDtypeStruct((m, n), F32),
        scratch_shapes=scratch,
        compiler_params=pltpu.CompilerParams(
            dimension_semantics=("parallel", "parallel", "arbitrary"),
            vmem_limit_bytes=VMEM_LIMIT),
        name=name,
    )(*args)


def _gate_up_kernel(h_ref, wg_ref, wu_ref, o_ref):
    h = h_ref[...]
    g = _dot(h, wg_ref[...])
    u = _dot(h, wu_ref[...])
    o_ref[...] = (_silu(g) * u).astype(o_ref.dtype)


def _gate_up(h, wg, wu, *, tm, tn):
    m, d = h.shape
    n = wg.shape[1]
    return pl.pallas_call(
        _gate_up_kernel,
        grid=(m // tm, n // tn),
        in_specs=[pl.BlockSpec((tm, d), lambda i, j: (i, 0)),
                  pl.BlockSpec((d, tn), lambda i, j: (0, j)),
                  pl.BlockSpec((d, tn), lambda i, j: (0, j))],
        out_specs=pl.BlockSpec((tm, tn), lambda i, j: (i, j)),
        out_shape=jax.ShapeDtypeStruct((m, n), BF16),
        compiler_params=pltpu.CompilerParams(dimension_semantics=("parallel", "parallel"),
                                             vmem_limit_bytes=VMEM_LIMIT),
        name="ffn_gate_up",
    )(h, wg, wu)


def _gdn_kernel(qkv_ref, z_ref, sm_ref, hist_ref, s0_ref, cw_ref, gv_ref, nw_ref,
                o_ref, hist_out_ref, s_ref, ext_ref, *, chunk, n_tiles):
    c = chunk
    t = pl.program_id(1)

    @pl.when(t == 0)
    def _():
        ext_ref[0:GDN_CARRY, :] = hist_ref[0]
        s_ref[...] = s0_ref[...]

    ext_ref[GDN_CARRY:GDN_CARRY + c, :] = qkv_ref[...]

    rows = lax.broadcasted_iota(jnp.int32, (c, c), 0)
    cols = lax.broadcasted_iota(jnp.int32, (c, c), 1)
    tril = rows >= cols
    strict = rows > cols
    tril_b = jnp.where(tril, 1.0, 0.0).astype(BF16)

    sm = sm_ref[...]
    beta_arr = jax.nn.sigmoid(sm)
    g_arr = -jnp.exp(gv_ref[0:1, :]) * _softplus(sm + gv_ref[1:2, :])
    gc_arr = _cumsum_rows(g_arr, tril_b)
    gc_t = gc_arr.T

    def conv(col0):
        acc = None
        for j in range(SHORT_CONV):
            r0 = GDN_CARRY - (SHORT_CONV - 1) + j
            term = ext_ref[r0:r0 + c, col0:col0 + LANES] * cw_ref[j:j + 1, col0:col0 + LANES]
            acc = term if acc is None else acc + term
        return acc

    for h in range(GDN_HEADS):
        lo = h * GDN_HEAD_DIM
        qh = _silu(conv(lo))
        kh = _silu(conv(GROUP_WIDTH + lo))
        vh = _silu(conv(2 * GROUP_WIDTH + lo))
        qn = qh * lax.rsqrt(jnp.sum(qh * qh, axis=-1, keepdims=True) + EPS) * (GDN_HEAD_DIM ** -0.5)
        kn = kh * lax.rsqrt(jnp.sum(kh * kh, axis=-1, keepdims=True) + EPS)
        beta = beta_arr[:, LANE_GB + h:LANE_GB + h + 1]
        gcol = gc_arr[:, LANE_GA + h:LANE_GA + h + 1]
        grow = gc_t[LANE_GA + h:LANE_GA + h + 1, :]
        glast = gc_arr[c - 1:c, LANE_GA + h:LANE_GA + h + 1]
        decay = jnp.where(tril, jnp.exp(jnp.where(tril, gcol - grow, 0.0)), 0.0)
        kb = kn * beta
        vb = vh * beta
        lm = jnp.where(strict, _dot_nt(kb, kn) * decay, 0.0)
        tinv = _unit_lower_inverse(lm, c)
        egc = jnp.exp(gcol)
        u = _dot3(tinv, vb)
        w = _dot3(tinv, kb * egc)
        attn = jnp.where(tril, _dot_nt(qn, kn) * decay, 0.0)
        s = s_ref[0, h]
        v_new = u - _dot(w, s)
        o = _dot(qn * egc, s) + _dot(attn, v_new)
        s_ref[0, h] = s * jnp.exp(glast) + _dot_tn(kn * jnp.exp(glast - gcol), v_new)
        zh = z_ref[:, lo:lo + GDN_HEAD_DIM]
        on = o * lax.rsqrt(jnp.mean(o * o, axis=-1, keepdims=True) + EPS) * nw_ref[...]
        o_ref[:, lo:lo + GDN_HEAD_DIM] = on * _silu(zh)

    @pl.when(t == n_tiles - 1)
    def _():
        hist_out_ref[0] = ext_ref[GDN_CARRY + c - (SHORT_CONV - 1):GDN_CARRY + c, :]

    if n_tiles > 1:
        ext_ref[0:GDN_CARRY, :] = ext_ref[c:c + GDN_CARRY, :]


def _gdn(proj, hist, s0, conv_w, a_log, dt_bias, norm_w, *, row0, batch, seq, chunk):
    n_tiles = seq // chunk
    blk0 = row0 // chunk
    rows = batch * seq
    w3 = 3 * GROUP_WIDTH
    hist_p = jnp.pad(hist, ((0, 0), (GDN_CARRY - (SHORT_CONV - 1), 0), (0, 0)))
    gv = jnp.zeros((SUBLANES, LANES), F32)
    gv = gv.at[0, LANE_GA:LANE_GA + GDN_HEADS].set(a_log).at[1, LANE_GA:LANE_GA + GDN_HEADS].set(dt_bias)
    row_blk = lambda b, t: blk0 + b * n_tiles + t
    return pl.pallas_call(
        functools.partial(_gdn_kernel, chunk=chunk, n_tiles=n_tiles),
        grid=(batch, n_tiles),
        in_specs=[
            pl.BlockSpec((chunk, w3), lambda b, t: (row_blk(b, t), COL_QKV // w3)),
            pl.BlockSpec((chunk, GROUP_WIDTH), lambda b, t: (row_blk(b, t), COL_GZ // GROUP_WIDTH)),
            pl.BlockSpec((chunk, LANES), lambda b, t: (row_blk(b, t), COL_SMALL // LANES)),
            pl.BlockSpec((1, GDN_CARRY, w3), lambda b, t: (b, 0, 0)),
            pl.BlockSpec((1, GDN_HEADS, GDN_HEAD_DIM, GDN_HEAD_DIM), lambda b, t: (b, 0, 0, 0)),
            pl.BlockSpec((SHORT_CONV, w3), lambda b, t: (0, 0)),
            pl.BlockSpec((SUBLANES, LANES), lambda b, t: (0, 0)),
            pl.BlockSpec((1, GDN_HEAD_DIM), lambda b, t: (0, 0)),
        ],
        out_specs=[
            pl.BlockSpec((chunk, GROUP_WIDTH), lambda b, t: (b * n_tiles + t, 0)),
            pl.BlockSpec((1, SHORT_CONV - 1, w3), lambda b, t: (b, 0, 0)),
            pl.BlockSpec((1, GDN_HEADS, GDN_HEAD_DIM, GDN_HEAD_DIM), lambda b, t: (b, 0, 0, 0)),
        ],
        out_shape=[
            jax.ShapeDtypeStruct((rows, GROUP_WIDTH), F32),
            jax.ShapeDtypeStruct((batch, SHORT_CONV - 1, w3), F32),
            jax.ShapeDtypeStruct((batch, GDN_HEADS, GDN_HEAD_DIM, GDN_HEAD_DIM), F32),
        ],
        scratch_shapes=[pltpu.VMEM((GDN_CARRY + chunk, w3), F32)],
        compiler_params=pltpu.CompilerParams(dimension_semantics=("parallel", "arbitrary"),
                                             vmem_limit_bytes=VMEM_LIMIT),
        name="gdn_mixer",
    )(proj, proj, proj, hist_p, s0, conv_w, gv, norm_w.reshape(1, GDN_HEAD_DIM))


def _ssm_kernel(z_ref, x_ref, bc_ref, sm_ref, hist_ref, h0_ref, cw_ref, cb_ref, sv_ref, dskip_ref, nw_ref,
                o_ref, hist_out_ref, h_ref, ext_ref, *, chunk, n_tiles):
    c = chunk
    t = pl.program_id(1)
    n_pairs = SSM_HEADS // 2
    pairs_per_group = n_pairs // SSM_GROUPS

    @pl.when(t == 0)
    def _():
        ext_ref[0:GDN_CARRY, :] = hist_ref[0]
        h_ref[...] = h0_ref[...]

    ext_ref[GDN_CARRY:GDN_CARRY + c, 0:GROUP_WIDTH] = x_ref[...]
    ext_ref[GDN_CARRY:GDN_CARRY + c, GROUP_WIDTH:SSM_CONV_DIM] = bc_ref[...]

    rows = lax.broadcasted_iota(jnp.int32, (c, c), 0)
    cols = lax.broadcasted_iota(jnp.int32, (c, c), 1)
    causal = rows >= cols
    tril_b = jnp.where(causal, 1.0, 0.0).astype(BF16)
    lane_lo = lax.broadcasted_iota(jnp.int32, (c, LANES), 1) < SSM_HEAD_DIM
    row_lo = lax.broadcasted_iota(jnp.int32, (2 * SSM_HEAD_DIM, SSM_STATE), 0) < SSM_HEAD_DIM

    sm = sm_ref[...]
    dt_arr = _softplus(sm + sv_ref[1:2, :])
    a_arr = dt_arr * (-jnp.exp(sv_ref[0:1, :]))
    acum = _cumsum_rows(a_arr, tril_b)
    acum_t = acum.T
    dt_t = dt_arr.T

    def conv_act(blk):
        col0 = blk * LANES
        acc = cb_ref[:, col0:col0 + LANES]
        for j in range(SHORT_CONV):
            r0 = GDN_CARRY - (SHORT_CONV - 1) + j
            acc = acc + ext_ref[r0:r0 + c, col0:col0 + LANES] * cw_ref[j:j + 1, col0:col0 + LANES]
        return _silu(acc)

    x_blocks = GROUP_WIDTH // LANES
    bm = [conv_act(x_blocks + g) for g in range(SSM_GROUPS)]
    cm = [conv_act(x_blocks + SSM_GROUPS + g) for g in range(SSM_GROUPS)]
    cbm = [_dot_nt(cm[g], bm[g]) for g in range(SSM_GROUPS)]

    ys = []
    for p in range(n_pairs):
        g = p // pairs_per_group
        xp = conv_act(p)
        yd, cs, ecol, cdec = [], [], [], []
        for hh in range(2):
            lane = LANE_DT + 2 * p + hh
            col = acum[:, lane:lane + 1]
            row = acum_t[lane:lane + 1, :]
            dtrow = dt_t[lane:lane + 1, :]
            dtcol = dt_arr[:, lane:lane + 1]
            alast = acum[c - 1:c, lane:lane + 1]
            lmat = jnp.where(causal, jnp.exp(jnp.where(causal, col - row, 0.0)), 0.0)
            yd.append(_dot(cbm[g] * lmat * dtrow, xp))
            cs.append(_dot_tn(xp * (jnp.exp(alast - col) * dtcol), bm[g]))
            ecol.append(jnp.exp(col))
            cdec.append(jnp.exp(alast))
        hp = h_ref[0, p]
        y = jnp.where(lane_lo, yd[0], yd[1])
        y = y + _dot_nt(cm[g], hp) * jnp.where(lane_lo, ecol[0], ecol[1])
        h_ref[0, p] = hp * jnp.where(row_lo, cdec[0], cdec[1]) + jnp.where(row_lo, cs[0], cs[1])
        y = y + dskip_ref[:, p * LANES:(p + 1) * LANES] * xp
        ys.append(y * _silu(z_ref[:, p * LANES:(p + 1) * LANES]))

    group_w = GROUP_WIDTH // SSM_GROUPS
    for g in range(SSM_GROUPS):
        blocks = ys[g * pairs_per_group:(g + 1) * pairs_per_group]
        ssq = None
        for y in blocks:
            s = jnp.sum(y * y, axis=-1, keepdims=True)
            ssq = s if ssq is None else ssq + s
        scale = lax.rsqrt(ssq / group_w + EPS)
        for i, y in enumerate(blocks):
            col0 = (g * pairs_per_group + i) * LANES
            o_ref[:, col0:col0 + LANES] = y * scale * nw_ref[:, col0:col0 + LANES]

    @pl.when(t == n_tiles - 1)
    def _():
        hist_out_ref[0] = ext_ref[GDN_CARRY + c - (SHORT_CONV - 1):GDN_CARRY + c, :]

    if n_tiles > 1:
        ext_ref[0:GDN_CARRY, :] = ext_ref[c:c + GDN_CARRY, :]


def _ssm(proj, hist, h0, conv_w, conv_b, a_log, dt_bias, d_skip, norm_w, *, row0, batch, seq, chunk):
    n_tiles = seq // chunk
    blk0 = row0 // chunk
    rows = batch * seq
    n_pairs = SSM_HEADS // 2
    bc_w = SSM_CONV_DIM - GROUP_WIDTH
    hist_p = jnp.pad(hist, ((0, 0), (GDN_CARRY - (SHORT_CONV - 1), 0), (0, 0)))
    sv = jnp.zeros((SUBLANES, LANES), F32)
    sv = sv.at[0, LANE_DT:LANE_DT + SSM_HEADS].set(a_log).at[1, LANE_DT:LANE_DT + SSM_HEADS].set(dt_bias)
    state_shape = (batch, n_pairs, 2 * SSM_HEAD_DIM, SSM_STATE)
    row_blk = lambda b, t: blk0 + b * n_tiles + t
    state_spec = pl.BlockSpec((1,) + state_shape[1:], lambda b, t: (b, 0, 0, 0))
    out, hist_new, h_new = pl.pallas_call(
        functools.partial(_ssm_kernel, chunk=chunk, n_tiles=n_tiles),
        grid=(batch, n_tiles),
        in_specs=[
            pl.BlockSpec((chunk, GROUP_WIDTH), lambda b, t: (row_blk(b, t), COL_SZ // GROUP_WIDTH)),
            pl.BlockSpec((chunk, GROUP_WIDTH), lambda b, t: (row_blk(b, t), COL_X // GROUP_WIDTH)),
            pl.BlockSpec((chunk, bc_w), lambda b, t: (row_blk(b, t), COL_BC // bc_w)),
            pl.BlockSpec((chunk, LANES), lambda b, t: (row_blk(b, t), COL_SMALL // LANES)),
            pl.BlockSpec((1, GDN_CARRY, SSM_CONV_DIM), lambda b, t: (b, 0, 0)),
            state_spec,
            pl.BlockSpec((SHORT_CONV, SSM_CONV_DIM), lambda b, t: (0, 0)),
            pl.BlockSpec((1, SSM_CONV_DIM), lambda b, t: (0, 0)),
            pl.BlockSpec((SUBLANES, LANES), lambda b, t: (0, 0)),
            pl.BlockSpec((1, GROUP_WIDTH), lambda b, t: (0, 0)),
            pl.BlockSpec((1, GROUP_WIDTH), lambda b, t: (0, 0)),
        ],
        out_specs=[
            pl.BlockSpec((chunk, GROUP_WIDTH), lambda b, t: (b * n_tiles + t, 0)),
            pl.BlockSpec((1, SHORT_CONV - 1, SSM_CONV_DIM), lambda b, t: (b, 0, 0)),
            state_spec,
        ],
        out_shape=[
            jax.ShapeDtypeStruct((rows, GROUP_WIDTH), F32),
            jax.ShapeDtypeStruct((batch, SHORT_CONV - 1, SSM_CONV_DIM), F32),
            jax.ShapeDtypeStruct(state_shape, F32),
        ],
        scratch_shapes=[pltpu.VMEM((GDN_CARRY + chunk, SSM_CONV_DIM), F32)],
        compiler_params=pltpu.CompilerParams(dimension_semantics=("parallel", "arbitrary"),
                                             vmem_limit_bytes=VMEM_LIMIT),
        name="ssm_mixer",
    )(proj, proj, proj, proj, hist_p, h0.reshape(state_shape), conv_w, conv_b.reshape(1, SSM_CONV_DIM), sv,
      jnp.repeat(d_skip, SSM_HEAD_DIM).reshape(1, GROUP_WIDTH), norm_w.reshape(1, GROUP_WIDTH))
    return out, hist_new, h_new.reshape(batch, SSM_HEADS, SSM_HEAD_DIM, SSM_STATE)


def _conf_kernel(a_ref, g_ref, hist_ref, w_ref, b_ref, lnw_ref, lnb_ref,
                 o_ref, hist_out_ref, ext_ref, *, tile, sub, n_tiles):
    t = pl.program_id(1)
    khist = CONF_KERNEL - 1

    @pl.when(t == 0)
    def _():
        ext_ref[0:CONF_CARRY, :] = hist_ref[0]

    ext_ref[CONF_CARRY:CONF_CARRY + tile, :] = a_ref[...] * jax.nn.sigmoid(g_ref[...])

    for cb in range(GROUP_WIDTH // LANES):
        col0 = cb * LANES
        for rb in range(tile // sub):
            acc = b_ref[:, col0:col0 + LANES]
            for j in range(CONF_KERNEL):
                r0 = CONF_CARRY - khist + j + rb * sub
                acc = acc + ext_ref[r0:r0 + sub, col0:col0 + LANES] * w_ref[j:j + 1, col0:col0 + LANES]
            o_ref[rb * sub:(rb + 1) * sub, col0:col0 + LANES] = acc

    cv = o_ref[...]
    mu = jnp.mean(cv, axis=-1, keepdims=True)
    xc = cv - mu
    y = xc * lax.rsqrt(jnp.mean(xc * xc, axis=-1, keepdims=True) + EPS) * lnw_ref[...] + lnb_ref[...]
    o_ref[...] = _silu(y)

    @pl.when(t == n_tiles - 1)
    def _():
        hist_out_ref[0] = ext_ref[CONF_CARRY + tile - khist:CONF_CARRY + tile, :]

    if n_tiles > 1:
        ext_ref[0:CONF_CARRY, :] = ext_ref[tile:tile + CONF_CARRY, :]


def _conf(proj, hist, dw_w, dw_b, ln_w, ln_b, *, row0, batch, seq, tile):
    n_tiles = seq // tile
    blk0 = row0 // tile
    rows = batch * seq
    khist = CONF_KERNEL - 1
    hist_p = jnp.pad(hist, ((0, 0), (CONF_CARRY - khist, 0), (0, 0)))
    row_blk = lambda b, t: blk0 + b * n_tiles + t
    vec = lambda v: v.reshape(1, GROUP_WIDTH)
    vec_spec = pl.BlockSpec((1, GROUP_WIDTH), lambda b, t: (0, 0))
    return pl.pallas_call(
        functools.partial(_conf_kernel, tile=tile, sub=min(tile, 64), n_tiles=n_tiles),
        grid=(batch, n_tiles),
        in_specs=[
            pl.BlockSpec((tile, GROUP_WIDTH), lambda b, t: (row_blk(b, t), COL_GLU_A // GROUP_WIDTH)),
            pl.BlockSpec((tile, GROUP_WIDTH), lambda b, t: (row_blk(b, t), COL_GLU_G // GROUP_WIDTH)),
            pl.BlockSpec((1, CONF_CARRY, GROUP_WIDTH), lambda b, t: (b, 0, 0)),
            pl.BlockSpec((CONF_KERNEL, GROUP_WIDTH), lambda b, t: (0, 0)),
            vec_spec, vec_spec, vec_spec,
        ],
        out_specs=[
            pl.BlockSpec((tile, GROUP_WIDTH), lambda b, t: (b * n_tiles + t, 0)),
            pl.BlockSpec((1, khist, GROUP_WIDTH), lambda b, t: (b, 0, 0)),
        ],
        out_shape=[
            jax.ShapeDtypeStruct((rows, GROUP_WIDTH), F32),
            jax.ShapeDtypeStruct((batch, khist, GROUP_WIDTH), F32),
        ],
        scratch_shapes=[pltpu.VMEM((CONF_CARRY + tile, GROUP_WIDTH), F32)],
        compiler_params=pltpu.CompilerParams(dimension_semantics=("parallel", "arbitrary"),
                                             vmem_limit_bytes=VMEM_LIMIT),
        name="conformer_mixer",
    )(proj, proj, hist_p, dw_w, vec(dw_b), vec(ln_w), vec(ln_b))


def _pool_kernel(u_ref, hist_ref, pw_ref, ps_ref, o_ref, hist_out_ref, ext_ref, *, tile, n_tiles, pos0):
    t = pl.program_id(1)

    @pl.when(t == 0)
    def _():
        ext_ref[0:POOL_CARRY, :] = hist_ref[0]

    ext_ref[POOL_CARRY:POOL_CARRY + tile, :] = u_ref[...]
    pos = (pos0 + t * tile + lax.broadcasted_iota(jnp.int32, (tile, 1), 0)).astype(F32)

    for gi, win in enumerate(POOL_WINDOWS):
        col0 = gi * POOL_GROUP
        acc = None
        for i in range(win):
            term = ext_ref[POOL_CARRY - i:POOL_CARRY - i + tile, col0:col0 + POOL_GROUP]
            acc = term if acc is None else acc + term
        cnt = jnp.minimum(jnp.float32(win), pos + 1.0)
        pooled = acc / cnt - u_ref[:, col0:col0 + POOL_GROUP]
        y = _dot(pooled.astype(BF16), pw_ref[gi])
        o_ref[:, col0:col0 + POOL_GROUP] = y * ps_ref[:, col0:col0 + POOL_GROUP]

    @pl.when(t == n_tiles - 1)
    def _():
        hist_out_ref[0] = ext_ref[POOL_CARRY + tile - POOL_HIST:POOL_CARRY + tile, :]

    if n_tiles > 1:
        ext_ref[0:POOL_CARRY, :] = ext_ref[tile:tile + POOL_CARRY, :]


def _pool(proj, hist, pool_w, pool_scale, *, row0, batch, seq, tile, pos0):
    n_tiles = seq // tile
    blk0 = row0 // tile
    rows = batch * seq
    n_win = len(POOL_WINDOWS)
    hist_p = jnp.pad(hist, ((0, 0), (POOL_CARRY - POOL_HIST, 0), (0, 0)))
    row_blk = lambda b, t: blk0 + b * n_tiles + t
    return pl.pallas_call(
        functools.partial(_pool_kernel, tile=tile, n_tiles=n_tiles, pos0=pos0),
        grid=(batch, n_tiles),
        in_specs=[
            pl.BlockSpec((tile, GROUP_WIDTH), lambda b, t: (row_blk(b, t), COL_POOL // GROUP_WIDTH)),
            pl.BlockSpec((1, POOL_CARRY, GROUP_WIDTH), lambda b, t: (b, 0, 0)),
            pl.BlockSpec((n_win, POOL_GROUP, POOL_GROUP), lambda b, t: (0, 0, 0)),
            pl.BlockSpec((1, GROUP_WIDTH), lambda b, t: (0, 0)),
        ],
        out_specs=[
            pl.BlockSpec((tile, GROUP_WIDTH), lambda b, t: (b * n_tiles + t, 0)),
            pl.BlockSpec((1, POOL_HIST, GROUP_WIDTH), lambda b, t: (b, 0, 0)),
        ],
        out_shape=[
            jax.ShapeDtypeStruct((rows, GROUP_WIDTH), F32),
            jax.ShapeDtypeStruct((batch, POOL_HIST, GROUP_WIDTH), F32),
        ],
        scratch_shapes=[pltpu.VMEM((POOL_CARRY + tile, GROUP_WIDTH), F32)],
        compiler_params=pltpu.CompilerParams(dimension_semantics=("parallel", "arbitrary"),
                                             vmem_limit_bytes=VMEM_LIMIT),
        name="pool_mixer",
    )(proj, hist_p, pool_w.astype(BF16), pool_scale.reshape(1, GROUP_WIDTH))


def _reorder_w_in(w_in):
    seg = lambda start, size: w_in[..., start:start + size]
    pad = jnp.zeros(w_in.shape[:-1] + (IN_COLS_PAD - COL_SMALL - 32,), w_in.dtype)
    return jnp.concatenate([
        seg(SRC_QKV, 3072), seg(SRC_GZ, 1024), seg(SRC_SZ, 1024), seg(SRC_POOL, 1024), seg(SRC_GLU, 2048),
        seg(SRC_XBC, 1536), seg(SRC_GB, 8), seg(SRC_GA, 8), seg(SRC_DT, 16), pad], axis=-1).astype(BF16)


def kernel(x_prompt, x_sample, state_gdn, state_gdn_conv, state_ssm, state_ssm_conv, state_conv, state_pool,
           norm_mix, w_in, gdn_conv_w, gdn_a_log, gdn_dt_bias, gdn_norm_w,
           ssm_conv_w, ssm_conv_b, ssm_a_log, ssm_dt_bias, ssm_d, ssm_norm_w,
           conf_dw_w, conf_dw_b, conf_ln_w, conf_ln_b, pool_w, pool_scale,
           w_out, norm_ffn, w_gate, w_up, w_down, norm_final):
    nb, seq, d = x_prompt.shape
    db, dseq, _ = x_sample.shape
    rows_p = nb * seq
    rows_s = db * dseq

    w_in_b = _reorder_w_in(w_in)
    w_out_b = w_out.astype(BF16)
    ff_pad = D_FF_PAD - D_FF
    w_gate_b = jnp.pad(w_gate, ((0, 0), (0, 0), (0, ff_pad))).astype(BF16)
    w_up_b = jnp.pad(w_up, ((0, 0), (0, 0), (0, ff_pad))).astype(BF16)
    w_down_b = jnp.pad(w_down, ((0, 0), (0, ff_pad), (0, 0))).astype(BF16)

    x = jnp.concatenate([x_prompt.reshape(rows_p, d), x_sample.reshape(rows_s, d)], axis=0)

    paths = ((0, nb, seq, 64, 128, 0), (rows_p, db, dseq, dseq, dseq, PAST_LEN))
    zeros_like_state = lambda s: jnp.zeros((nb,) + s.shape[2:], s.dtype)
    new_states = [[[] for _ in range(6)] for _ in paths]

    for l in range(DEPTH):
        h = _rmsnorm(x, norm_mix[l], BF16)
        proj = _matmul(h, w_in_b[l], tm=1024, tn=768, tk=d, name="in_proj")
        outs = []
        for pi, (row0, batch, slen, chunk, tile, pos0) in enumerate(paths):
            if pi == 0:
                st = [zeros_like_state(s) for s in (state_gdn, state_gdn_conv, state_ssm, state_ssm_conv,
                                                    state_conv, state_pool)]
            else:
                st = [s[l] for s in (state_gdn, state_gdn_conv, state_ssm, state_ssm_conv,
                                     state_conv, state_pool)]
            o_gdn, n_gdn_conv, n_gdn = _gdn(proj, st[1], st[0], gdn_conv_w[l], gdn_a_log[l], gdn_dt_bias[l],
                                            gdn_norm_w[l], row0=row0, batch=batch, seq=slen, chunk=chunk)
            o_ssm, n_ssm_conv, n_ssm = _ssm(proj, st[3], st[2], ssm_conv_w[l], ssm_conv_b[l], ssm_a_log[l],
                                            ssm_dt_bias[l], ssm_d[l], ssm_norm_w[l],
                                            row0=row0, batch=batch, seq=slen, chunk=chunk)
            o_conv, n_conv = _conf(proj, st[4], conf_dw_w[l], conf_dw_b[l], conf_ln_w[l], conf_ln_b[l],
                                   row0=row0, batch=batch, seq=slen, tile=tile)
            o_pool, n_pool = _pool(proj, st[5], pool_w[l], pool_scale[l],
                                   row0=row0, batch=batch, seq=slen, tile=tile, pos0=pos0)
            outs.append(jnp.concatenate([o_gdn, o_ssm, o_conv, o_pool], axis=-1))
            for i, s in enumerate((n_gdn, n_gdn_conv, n_ssm, n_ssm_conv, n_conv, n_pool)):
                new_states[pi][i].append(s)
        mix = jnp.concatenate(outs, axis=0).astype(BF16)
        x = _matmul(mix, w_out_b[l], x, tm=1024, tn=1024, tk=d, name="out_proj")
        h = _rmsnorm(x, norm_ffn[l], BF16)
        ff = _gate_up(h, w_gate_b[l], w_up_b[l], tm=1024, tn=512)
        x = _matmul(ff, w_down_b[l], x, tm=1024, tn=1024, tk=D_FF_PAD // 4, name="ffn_down")

    y_prompt = _rmsnorm(x, norm_final, F32, row0=0, rows=rows_p).reshape(nb, seq, d)
    y_sample = _rmsnorm(x, norm_final, F32, row0=rows_p, rows=rows_s).reshape(db, dseq, d)
    stacked = [tuple(jnp.stack(c, axis=0) for c in new_states[pi]) for pi in range(len(paths))]
    return (y_prompt, y_sample) + stacked[0] + stacked[1]
```

```python
import functools

import jax
import jax.numpy as jnp
from jax import lax
from jax.experimental import pallas as pl
from jax.experimental.pallas import tpu as pltpu

F32 = jnp.float32
BF16 = jnp.bfloat16

D_MODEL = 4096
DEPTH = 4
PAST_LEN = 16384
GROUP_WIDTH = 1024
GDN_HEADS = 8
GDN_HEAD_DIM = 128
SHORT_CONV = 4
SSM_HEADS = 16
SSM_HEAD_DIM = 64
SSM_GROUPS = 2
SSM_STATE = 128
SSM_CONV_DIM = 1536
CONF_KERNEL = 31
POOL_WINDOWS = (2, 4, 8, 16)
POOL_GROUP = 256
POOL_HIST = 15
D_FF = 11008
EPS = 1e-6

SUBLANES = 8
LANES = 128
BF16_ROWS = 16
MXU_DIM = 256
VMEM_LIMIT = 56 * 1024 * 1024

COL_QKV = 0
COL_GZ = 3072
COL_SZ = 4096
COL_POOL = 5120
COL_GLU_A = 6144
COL_GLU_G = 7168
COL_X = 8192
COL_BC = 9216
COL_SMALL = 9728
LANE_GB = 0
LANE_GA = 8
LANE_DT = 16
IN_COLS_PAD = 9984

SRC_QKV, SRC_GZ, SRC_GB, SRC_GA, SRC_SZ, SRC_XBC, SRC_DT, SRC_GLU, SRC_POOL = (
    0, 3072, 4096, 4104, 4112, 5136, 6672, 6688, 8736)

CONV_CARRY = 8
CONF_CARRY = 32
POOL_CARRY = 16


def _mxu_operands(a, b):
    if a.shape[0] % BF16_ROWS == 0 and b.shape[0] % BF16_ROWS == 0:
        return a.astype(BF16), b.astype(BF16)
    return a.astype(F32), b.astype(F32)


def _dot(a, b):
    a, b = _mxu_operands(a, b)
    return jnp.dot(a, b, preferred_element_type=F32)


def _dot_nt(a, b):
    a, b = _mxu_operands(a, b)
    return lax.dot_general(a, b, (((1,), (1,)), ((), ())), preferred_element_type=F32)


def _dot_tn(a, b):
    a, b = _mxu_operands(a, b)
    return lax.dot_general(a, b, (((0,), (0,)), ((), ())), preferred_element_type=F32)


def _rows_dot_exact(sel, x):
    hi = x.astype(BF16)
    r = x - hi.astype(F32)
    mid = r.astype(BF16)
    lo = (r - mid.astype(F32)).astype(BF16)
    dot = lambda v: jnp.dot(sel, v, preferred_element_type=F32)
    return dot(hi) + (dot(mid) + dot(lo))


def _block_masks(n, shift):
    r = lax.broadcasted_iota(jnp.int32, (n, n), 0)
    q = lax.broadcasted_iota(jnp.int32, (n, n), 1)
    same = (r >> shift) == (q >> shift)
    return same, jnp.logical_and(same, r >= q), jnp.logical_and(same, r > q)


def _silu(x):
    return x * jax.nn.sigmoid(x)


def _softplus(x):
    return jnp.maximum(x, 0.0) + jnp.log1p(jnp.exp(-jnp.abs(x)))


def _rmsnorm_kernel(x_ref, w_ref, o_ref):
    x = x_ref[...]
    ms = jnp.mean(x * x, axis=-1, keepdims=True)
    o_ref[...] = (x * lax.rsqrt(ms + EPS) * w_ref[...]).astype(o_ref.dtype)


def _rmsnorm(x, w, out_dtype, *, row0=0, rows=None, tm=256):
    t, d = x.shape
    rows = t if rows is None else rows
    blk0 = row0 // tm
    return pl.pallas_call(
        _rmsnorm_kernel,
        grid=(rows // tm,),
        in_specs=[pl.BlockSpec((tm, d), lambda i: (blk0 + i, 0)),
                  pl.BlockSpec((1, d), lambda i: (0, 0))],
        out_specs=pl.BlockSpec((tm, d), lambda i: (i, 0)),
        out_shape=jax.ShapeDtypeStruct((rows, d), out_dtype),
        compiler_params=pltpu.CompilerParams(dimension_semantics=("parallel",),
                                             vmem_limit_bytes=VMEM_LIMIT),
        name="rmsnorm",
    )(x, w.reshape(1, d))


def _matmul_kernel(a_ref, b_ref, *rest, has_res):
    part = jnp.dot(a_ref[...], b_ref[...], preferred_element_type=F32)
    if has_res:
        r_ref, o_ref = rest
        o_ref[...] = part + r_ref[...]
    else:
        rest[0][...] = part


def _matmul(a, b, res=None, *, tm, tn, name, row0=0, rows=None):
    kdim = a.shape[1]
    rows = a.shape[0] if rows is None else rows
    n = b.shape[1]
    blk0 = row0 // tm
    in_specs = [pl.BlockSpec((tm, kdim), lambda i, j: (blk0 + i, 0)),
                pl.BlockSpec((kdim, tn), lambda i, j: (0, j))]
    args = [a, b]
    if res is not None:
        in_specs.append(pl.BlockSpec((tm, tn), lambda i, j: (i, j)))
        args.append(res)
    return pl.pallas_call(
        functools.partial(_matmul_kernel, has_res=res is not None),
        grid=(rows // tm, n // tn),
        in_specs=in_specs,
        out_specs=pl.BlockSpec((tm, tn), lambda i, j: (i, j)),
        out_shape=jax.ShapeDtypeStruct((rows, n), F32),
        compiler_params=pltpu.CompilerParams(dimension_semantics=("parallel", "parallel"),
                                             vmem_limit_bytes=VMEM_LIMIT),
        name=name,
    )(*args)


def _gate_up_kernel(h_ref, wg_ref, wu_ref, o_ref, wg_bf_ref, wu_bf_ref):
    @pl.when(pl.program_id(1) == 0)
    def _():
        wg_bf_ref[...] = wg_ref[...].astype(BF16)
        wu_bf_ref[...] = wu_ref[...].astype(BF16)

    h = h_ref[...]
    g = jnp.dot(h, wg_bf_ref[...], preferred_element_type=F32)
    u = jnp.dot(h, wu_bf_ref[...], preferred_element_type=F32)
    o_ref[...] = (_silu(g) * u).astype(o_ref.dtype)


def _gate_up(h, wg, wu, *, tm, tn):
    m, d = h.shape
    n = wg.shape[1]
    w_spec = pl.BlockSpec((d, tn), lambda j, i: (0, j))
    return pl.pallas_call(
        _gate_up_kernel,
        grid=(n // tn, m // tm),
        in_specs=[pl.BlockSpec((tm, d), lambda j, i: (i, 0)), w_spec, w_spec],
        out_specs=pl.BlockSpec((tm, tn), lambda j, i: (i, j)),
        out_shape=jax.ShapeDtypeStruct((m, n), BF16),
        scratch_shapes=[pltpu.VMEM((d, tn), BF16), pltpu.VMEM((d, tn), BF16)],
        compiler_params=pltpu.CompilerParams(dimension_semantics=("parallel", "arbitrary"),
                                             vmem_limit_bytes=VMEM_LIMIT),
        name="ffn_gate_up",
    )(h, wg, wu)


def _gdn_kernel(qkv_ref, z_ref, sm_ref, hist_ref, s0_ref, cw_ref, gv_ref, nw_ref,
                o_ref, hist_out_ref, s_ref, ext_ref, *, chunk, nseq, n_tiles, hpg):
    c = chunk
    rows = nseq * c
    gr = hpg * rows
    shift = c.bit_length() - 1
    t = pl.program_id(1)

    @pl.when(t == 0)
    def _():
        ext_ref[:, 0:CONV_CARRY, :] = hist_ref[...]
        s_ref[...] = s0_ref[...]

    ext_ref[:, CONV_CARRY:CONV_CARRY + c, :] = qkv_ref[...]

    same_t, tril_t, _ = _block_masks(rows, shift)
    _, tril_g, strict_g = _block_masks(gr, shift)

    sm = sm_ref[...].reshape(rows, LANES)
    beta_arr = jax.nn.sigmoid(sm)
    g_arr = -jnp.exp(gv_ref[0:1, :]) * _softplus(sm + gv_ref[1:2, :])
    gc_arr = _rows_dot_exact(jnp.where(tril_t, 1.0, 0.0).astype(BF16), g_arr)
    gl_arr = _rows_dot_exact(jnp.where(same_t, 1.0, 0.0).astype(BF16), g_arr)

    def conv(s, col0):
        acc = None
        for j in range(SHORT_CONV):
            r0 = CONV_CARRY - (SHORT_CONV - 1) + j
            term = ext_ref[s, r0:r0 + c, col0:col0 + LANES] * cw_ref[j:j + 1, col0:col0 + LANES]
            acc = term if acc is None else acc + term
        return acc

    for grp in range(GDN_HEADS // hpg):
        heads = range(grp * hpg, (grp + 1) * hpg)
        pairs = [(h, s) for h in heads for s in range(nseq)]
        stack = lambda fn: jnp.concatenate([fn(h, s) for h, s in pairs], axis=0)
        column = lambda arr, lane0: jnp.concatenate(
            [arr[:, lane0 + h:lane0 + h + 1] for h in heads], axis=0)

        qh = _silu(stack(lambda h, s: conv(s, h * GDN_HEAD_DIM)))
        kh = _silu(stack(lambda h, s: conv(s, GROUP_WIDTH + h * GDN_HEAD_DIM)))
        vh = _silu(stack(lambda h, s: conv(s, 2 * GROUP_WIDTH + h * GDN_HEAD_DIM)))
        zs = stack(lambda h, s: z_ref[s, :, h * GDN_HEAD_DIM:(h + 1) * GDN_HEAD_DIM])
        qn = qh * lax.rsqrt(jnp.sum(qh * qh, axis=-1, keepdims=True) + EPS) * (GDN_HEAD_DIM ** -0.5)
        kn = kh * lax.rsqrt(jnp.sum(kh * kh, axis=-1, keepdims=True) + EPS)
        beta = column(beta_arr, LANE_GB)
        gcol = column(gc_arr, LANE_GA)
        glast = column(gl_arr, LANE_GA)
        grow = jnp.broadcast_to(gcol, (gr, LANES)).T[0:1, :]
        decay = jnp.where(tril_g, jnp.exp(jnp.where(tril_g, gcol - grow, 0.0)), 0.0)
        kb = kn * beta
        vb = vh * beta
        lm = jnp.where(strict_g, _dot_nt(kb, kn) * decay, 0.0)

        r = -lm
        p = r
        for _ in range(shift - 1):
            p = _dot(p, p)
            r = r + p + _dot(r, p)

        egc = jnp.exp(gcol)
        rhs = jnp.concatenate([vb, kb * egc], axis=1)
        sol = rhs + _dot(r, rhs)
        u = sol[:, :GDN_HEAD_DIM]
        w = sol[:, GDN_HEAD_DIM:]
        attn = jnp.where(tril_g, _dot_nt(qn, kn) * decay, 0.0)
        qe = qn * egc
        kdec = kn * jnp.exp(glast - gcol)

        v_parts, o_parts = [], []
        for i, (h, s) in enumerate(pairs):
            st = s_ref[s, h]
            v_parts.append(u[i * c:(i + 1) * c] - _dot(w[i * c:(i + 1) * c], st))
            o_parts.append(_dot(qe[i * c:(i + 1) * c], st))
        v_new = jnp.concatenate(v_parts, axis=0)
        o = jnp.concatenate(o_parts, axis=0) + _dot(attn, v_new)
        for i, (h, s) in enumerate(pairs):
            gl = gl_arr[s * c:s * c + 1, LANE_GA + h:LANE_GA + h + 1]
            s_ref[s, h] = (s_ref[s, h] * jnp.exp(gl)
                           + _dot_tn(kdec[i * c:(i + 1) * c], v_new[i * c:(i + 1) * c]))

        on = o * lax.rsqrt(jnp.mean(o * o, axis=-1, keepdims=True) + EPS) * nw_ref[...]
        res = on * _silu(zs)
        for i, (h, s) in enumerate(pairs):
            o_ref[s, :, h * GDN_HEAD_DIM:(h + 1) * GDN_HEAD_DIM] = res[i * c:(i + 1) * c]

    @pl.when(t == n_tiles - 1)
    def _():
        hist_out_ref[...] = ext_ref[:, CONV_CARRY + c - (SHORT_CONV - 1):CONV_CARRY + c, :]

    if n_tiles > 1:
        ext_ref[:, 0:CONV_CARRY, :] = ext_ref[:, c:c + CONV_CARRY, :]


def _gdn(proj, hist, s0, conv_w, a_log, dt_bias, norm_w, *, chunk, nseq):
    batch, seq, _ = proj.shape
    n_tiles = seq // chunk
    hpg = MXU_DIM // (nseq * chunk)
    w3 = 3 * GROUP_WIDTH
    hist_p = jnp.pad(hist, ((0, 0), (CONV_CARRY - (SHORT_CONV - 1), 0), (0, 0)))
    gv = jnp.zeros((SUBLANES, LANES), F32)
    gv = gv.at[0, LANE_GA:LANE_GA + GDN_HEADS].set(a_log).at[1, LANE_GA:LANE_GA + GDN_HEADS].set(dt_bias)
    state_spec = pl.BlockSpec((nseq, GDN_HEADS, GDN_HEAD_DIM, GDN_HEAD_DIM), lambda g, t: (g, 0, 0, 0))
    return pl.pallas_call(
        functools.partial(_gdn_kernel, chunk=chunk, nseq=nseq, n_tiles=n_tiles, hpg=hpg),
        grid=(batch // nseq, n_tiles),
        in_specs=[
            pl.BlockSpec((nseq, chunk, w3), lambda g, t: (g, t, COL_QKV // w3)),
            pl.BlockSpec((nseq, chunk, GROUP_WIDTH), lambda g, t: (g, t, COL_GZ // GROUP_WIDTH)),
            pl.BlockSpec((nseq, chunk, LANES), lambda g, t: (g, t, COL_SMALL // LANES)),
            pl.BlockSpec((nseq, CONV_CARRY, w3), lambda g, t: (g, 0, 0)),
            state_spec,
            pl.BlockSpec((SHORT_CONV, w3), lambda g, t: (0, 0)),
            pl.BlockSpec((SUBLANES, LANES), lambda g, t: (0, 0)),
            pl.BlockSpec((1, GDN_HEAD_DIM), lambda g, t: (0, 0)),
        ],
        out_specs=[
            pl.BlockSpec((nseq, chunk, GROUP_WIDTH), lambda g, t: (g, t, 0)),
            pl.BlockSpec((nseq, SHORT_CONV - 1, w3), lambda g, t: (g, 0, 0)),
            state_spec,
        ],
        out_shape=[
            jax.ShapeDtypeStruct((batch, seq, GROUP_WIDTH), F32),
            jax.ShapeDtypeStruct((batch, SHORT_CONV - 1, w3), F32),
            jax.ShapeDtypeStruct((batch, GDN_HEADS, GDN_HEAD_DIM, GDN_HEAD_DIM), F32),
        ],
        scratch_shapes=[pltpu.VMEM((nseq, CONV_CARRY + chunk, w3), F32)],
        compiler_params=pltpu.CompilerParams(dimension_semantics=("parallel", "arbitrary"),
                                             vmem_limit_bytes=VMEM_LIMIT),
        name="gdn_mixer",
    )(proj, proj, proj, hist_p, s0, conv_w, gv, norm_w.reshape(1, GDN_HEAD_DIM))


def _ssm_kernel(z_ref, x_ref, bc_ref, sm_ref, hist_ref, h0_ref, cw_ref, cb_ref, sv_ref, dskip_ref, nw_ref,
                o_ref, hist_out_ref, h_ref, ext_ref, *, chunk, nseq, n_tiles):
    c = chunk
    t = pl.program_id(1)
    n_pairs = SSM_HEADS // 2
    pairs_per_group = n_pairs // SSM_GROUPS
    x_blocks = GROUP_WIDTH // LANES
    group_w = GROUP_WIDTH // SSM_GROUPS

    @pl.when(t == 0)
    def _():
        ext_ref[:, 0:CONV_CARRY, :] = hist_ref[...]
        h_ref[...] = h0_ref[...]

    ext_ref[:, CONV_CARRY:CONV_CARRY + c, 0:GROUP_WIDTH] = x_ref[...]
    ext_ref[:, CONV_CARRY:CONV_CARRY + c, GROUP_WIDTH:SSM_CONV_DIM] = bc_ref[...]

    rows = lax.broadcasted_iota(jnp.int32, (c, c), 0)
    cols = lax.broadcasted_iota(jnp.int32, (c, c), 1)
    causal = rows >= cols
    tril_b = jnp.where(causal, 1.0, 0.0).astype(BF16)
    lane_lo = lax.broadcasted_iota(jnp.int32, (c, LANES), 1) < SSM_HEAD_DIM
    row_lo = lax.broadcasted_iota(jnp.int32, (2 * SSM_HEAD_DIM, SSM_STATE), 0) < SSM_HEAD_DIM
    neg_a = -jnp.exp(sv_ref[0:1, :])

    for s in range(nseq):
        sm = sm_ref[s]
        dt_arr = _softplus(sm + sv_ref[1:2, :])
        acum = _rows_dot_exact(tril_b, dt_arr * neg_a)
        acum_t = acum.T
        dt_t = dt_arr.T

        def conv_act(blk):
            col0 = blk * LANES
            acc = cb_ref[:, col0:col0 + LANES]
            for j in range(SHORT_CONV):
                r0 = CONV_CARRY - (SHORT_CONV - 1) + j
                acc = acc + ext_ref[s, r0:r0 + c, col0:col0 + LANES] * cw_ref[j:j + 1, col0:col0 + LANES]
            return _silu(acc)

        bm = [conv_act(x_blocks + g) for g in range(SSM_GROUPS)]
        cm = [conv_act(x_blocks + SSM_GROUPS + g) for g in range(SSM_GROUPS)]
        cbm = [_dot_nt(cm[g], bm[g]) for g in range(SSM_GROUPS)]

        ys = []
        for p in range(n_pairs):
            g = p // pairs_per_group
            xp = conv_act(p)
            yd, cs, ecol, cdec = [], [], [], []
            for hh in range(2):
                lane = LANE_DT + 2 * p + hh
                col = acum[:, lane:lane + 1]
                row = acum_t[lane:lane + 1, :]
                dtrow = dt_t[lane:lane + 1, :]
                dtcol = dt_arr[:, lane:lane + 1]
                alast = acum[c - 1:c, lane:lane + 1]
                lmat = jnp.where(causal, jnp.exp(jnp.where(causal, col - row, 0.0)), 0.0)
                yd.append(_dot(cbm[g] * lmat * dtrow, xp))
                cs.append(_dot_tn(xp * (jnp.exp(alast - col) * dtcol), bm[g]))
                ecol.append(jnp.exp(col))
                cdec.append(jnp.exp(alast))
            hp = h_ref[s, p]
            y = jnp.where(lane_lo, yd[0], yd[1])
            y = y + _dot_nt(cm[g], hp) * jnp.where(lane_lo, ecol[0], ecol[1])
            h_ref[s, p] = hp * jnp.where(row_lo, cdec[0], cdec[1]) + jnp.where(row_lo, cs[0], cs[1])
            y = y + dskip_ref[:, p * LANES:(p + 1) * LANES] * xp
            ys.append(y * _silu(z_ref[s, :, p * LANES:(p + 1) * LANES]))

        for g in range(SSM_GROUPS):
            blocks = ys[g * pairs_per_group:(g + 1) * pairs_per_group]
            ssq = None
            for y in blocks:
                sq = jnp.sum(y * y, axis=-1, keepdims=True)
                ssq = sq if ssq is None else ssq + sq
            scale = lax.rsqrt(ssq / group_w + EPS)
            for i, y in enumerate(blocks):
                col0 = (g * pairs_per_group + i) * LANES
                o_ref[s, :, col0:col0 + LANES] = y * scale * nw_ref[:, col0:col0 + LANES]

    @pl.when(t == n_tiles - 1)
    def _():
        hist_out_ref[...] = ext_ref[:, CONV_CARRY + c - (SHORT_CONV - 1):CONV_CARRY + c, :]

    if n_tiles > 1:
        ext_ref[:, 0:CONV_CARRY, :] = ext_ref[:, c:c + CONV_CARRY, :]


def _ssm(proj, hist, h0, conv_w, conv_b, a_log, dt_bias, d_skip, norm_w, *, chunk, nseq):
    batch, seq, _ = proj.shape
    n_tiles = seq // chunk
    n_pairs = SSM_HEADS // 2
    bc_w = SSM_CONV_DIM - GROUP_WIDTH
    hist_p = jnp.pad(hist, ((0, 0), (CONV_CARRY - (SHORT_CONV - 1), 0), (0, 0)))
    sv = jnp.zeros((SUBLANES, LANES), F32)
    sv = sv.at[0, LANE_DT:LANE_DT + SSM_HEADS].set(a_log).at[1, LANE_DT:LANE_DT + SSM_HEADS].set(dt_bias)
    state_shape = (batch, n_pairs, 2 * SSM_HEAD_DIM, SSM_STATE)
    state_spec = pl.BlockSpec((nseq,) + state_shape[1:], lambda g, t: (g, 0, 0, 0))
    vec_spec = lambda width: pl.BlockSpec((1, width), lambda g, t: (0, 0))
    out, hist_new, h_new = pl.pallas_call(
        functools.partial(_ssm_kernel, chunk=chunk, nseq=nseq, n_tiles=n_tiles),
        grid=(batch // nseq, n_tiles),
        in_specs=[
            pl.BlockSpec((nseq, chunk, GROUP_WIDTH), lambda g, t: (g, t, COL_SZ // GROUP_WIDTH)),
            pl.BlockSpec((nseq, chunk, GROUP_WIDTH), lambda g, t: (g, t, COL_X // GROUP_WIDTH)),
            pl.BlockSpec((nseq, chunk, bc_w), lambda g, t: (g, t, COL_BC // bc_w)),
            pl.BlockSpec((nseq, chunk, LANES), lambda g, t: (g, t, COL_SMALL // LANES)),
            pl.BlockSpec((nseq, CONV_CARRY, SSM_CONV_DIM), lambda g, t: (g, 0, 0)),
            state_spec,
            pl.BlockSpec((SHORT_CONV, SSM_CONV_DIM), lambda g, t: (0, 0)),
            vec_spec(SSM_CONV_DIM),
            pl.BlockSpec((SUBLANES, LANES), lambda g, t: (0, 0)),
            vec_spec(GROUP_WIDTH),
            vec_spec(GROUP_WIDTH),
        ],
        out_specs=[
            pl.BlockSpec((nseq, chunk, GROUP_WIDTH), lambda g, t: (g, t, 0)),
            pl.BlockSpec((nseq, SHORT_CONV - 1, SSM_CONV_DIM), lambda g, t: (g, 0, 0)),
            state_spec,
        ],
        out_shape=[
            jax.ShapeDtypeStruct((batch, seq, GROUP_WIDTH), F32),
            jax.ShapeDtypeStruct((batch, SHORT_CONV - 1, SSM_CONV_DIM), F32),
            jax.ShapeDtypeStruct(state_shape, F32),
        ],
        scratch_shapes=[pltpu.VMEM((nseq, CONV_CARRY + chunk, SSM_CONV_DIM), F32)],
        compiler_params=pltpu.CompilerParams(dimension_semantics=("parallel", "arbitrary"),
                                             vmem_limit_bytes=VMEM_LIMIT),
        name="ssm_mixer",
    )(proj, proj, proj, proj, hist_p, h0.reshape(state_shape), conv_w, conv_b.reshape(1, SSM_CONV_DIM), sv,
      jnp.repeat(d_skip, SSM_HEAD_DIM).reshape(1, GROUP_WIDTH), norm_w.reshape(1, GROUP_WIDTH))
    return out, hist_new, h_new.reshape(batch, SSM_HEADS, SSM_HEAD_DIM, SSM_STATE)


def _conf_kernel(a_ref, g_ref, hist_ref, w_ref, b_ref, lnw_ref, lnb_ref,
                 o_ref, hist_out_ref, ext_ref, *, tile, sub, nseq, n_tiles):
    t = pl.program_id(1)
    khist = CONF_KERNEL - 1

    @pl.when(t == 0)
    def _():
        ext_ref[:, 0:CONF_CARRY, :] = hist_ref[...]

    ext_ref[:, CONF_CARRY:CONF_CARRY + tile, :] = a_ref[...] * jax.nn.sigmoid(g_ref[...])

    for s in range(nseq):
        for cb in range(GROUP_WIDTH // LANES):
            col0 = cb * LANES
            for rb in range(tile // sub):
                acc = b_ref[:, col0:col0 + LANES]
                for j in range(CONF_KERNEL):
                    r0 = CONF_CARRY - khist + j + rb * sub
                    acc = acc + ext_ref[s, r0:r0 + sub, col0:col0 + LANES] * w_ref[j:j + 1, col0:col0 + LANES]
                o_ref[s, rb * sub:(rb + 1) * sub, col0:col0 + LANES] = acc

    cv = o_ref[...]
    mu = jnp.mean(cv, axis=-1, keepdims=True)
    xc = cv - mu
    y = xc * lax.rsqrt(jnp.mean(xc * xc, axis=-1, keepdims=True) + EPS) * lnw_ref[...] + lnb_ref[...]
    o_ref[...] = _silu(y)

    @pl.when(t == n_tiles - 1)
    def _():
        hist_out_ref[...] = ext_ref[:, CONF_CARRY + tile - khist:CONF_CARRY + tile, :]

    if n_tiles > 1:
        ext_ref[:, 0:CONF_CARRY, :] = ext_ref[:, tile:tile + CONF_CARRY, :]


def _conf(proj, hist, dw_w, dw_b, ln_w, ln_b, *, tile, nseq):
    batch, seq, _ = proj.shape
    n_tiles = seq // tile
    khist = CONF_KERNEL - 1
    hist_p = jnp.pad(hist, ((0, 0), (CONF_CARRY - khist, 0), (0, 0)))
    vec = lambda v: v.reshape(1, GROUP_WIDTH)
    vec_spec = pl.BlockSpec((1, GROUP_WIDTH), lambda g, t: (0, 0))
    return pl.pallas_call(
        functools.partial(_conf_kernel, tile=tile, sub=min(tile, 64), nseq=nseq, n_tiles=n_tiles),
        grid=(batch // nseq, n_tiles),
        in_specs=[
            pl.BlockSpec((nseq, tile, GROUP_WIDTH), lambda g, t: (g, t, COL_GLU_A // GROUP_WIDTH)),
            pl.BlockSpec((nseq, tile, GROUP_WIDTH), lambda g, t: (g, t, COL_GLU_G // GROUP_WIDTH)),
            pl.BlockSpec((nseq, CONF_CARRY, GROUP_WIDTH), lambda g, t: (g, 0, 0)),
            pl.BlockSpec((CONF_KERNEL, GROUP_WIDTH), lambda g, t: (0, 0)),
            vec_spec, vec_spec, vec_spec,
        ],
        out_specs=[
            pl.BlockSpec((nseq, tile, GROUP_WIDTH), lambda g, t: (g, t, 0)),
            pl.BlockSpec((nseq, khist, GROUP_WIDTH), lambda g, t: (g, 0, 0)),
        ],
        out_shape=[
            jax.ShapeDtypeStruct((batch, seq, GROUP_WIDTH), F32),
            jax.ShapeDtypeStruct((batch, khist, GROUP_WIDTH), F32),
        ],
        scratch_shapes=[pltpu.VMEM((nseq, CONF_CARRY + tile, GROUP_WIDTH), F32)],
        compiler_params=pltpu.CompilerParams(dimension_semantics=("parallel", "arbitrary"),
                                             vmem_limit_bytes=VMEM_LIMIT),
        name="conformer_mixer",
    )(proj, proj, hist_p, dw_w, vec(dw_b), vec(ln_w), vec(ln_b))


def _pool_kernel(u_ref, hist_ref, pw_ref, ps_ref, o_ref, hist_out_ref, ext_ref, *, tile, nseq, n_tiles, pos0):
    t = pl.program_id(1)

    @pl.when(t == 0)
    def _():
        ext_ref[:, 0:POOL_CARRY, :] = hist_ref[...]

    ext_ref[:, POOL_CARRY:POOL_CARRY + tile, :] = u_ref[...]
    pos = (pos0 + t * tile + lax.broadcasted_iota(jnp.int32, (tile, 1), 0)).astype(F32)

    for s in range(nseq):
        for gi, win in enumerate(POOL_WINDOWS):
            col0 = gi * POOL_GROUP
            acc = None
            for i in range(win):
                term = ext_ref[s, POOL_CARRY - i:POOL_CARRY - i + tile, col0:col0 + POOL_GROUP]
                acc = term if acc is None else acc + term
            cnt = jnp.minimum(jnp.float32(win), pos + 1.0)
            pooled = acc / cnt - u_ref[s, :, col0:col0 + POOL_GROUP]
            y = _dot(pooled, pw_ref[gi])
            o_ref[s, :, col0:col0 + POOL_GROUP] = y * ps_ref[:, col0:col0 + POOL_GROUP]

    @pl.when(t == n_tiles - 1)
    def _():
        hist_out_ref[...] = ext_ref[:, POOL_CARRY + tile - POOL_HIST:POOL_CARRY + tile, :]

    if n_tiles > 1:
        ext_ref[:, 0:POOL_CARRY, :] = ext_ref[:, tile:tile + POOL_CARRY, :]


def _pool(proj, hist, pool_w, pool_scale, *, tile, nseq, pos0):
    batch, seq, _ = proj.shape
    n_tiles = seq // tile
    n_win = len(POOL_WINDOWS)
    hist_p = jnp.pad(hist, ((0, 0), (POOL_CARRY - POOL_HIST, 0), (0, 0)))
    return pl.pallas_call(
        functools.partial(_pool_kernel, tile=tile, nseq=nseq, n_tiles=n_tiles, pos0=pos0),
        grid=(batch // nseq, n_tiles),
        in_specs=[
            pl.BlockSpec((nseq, tile, GROUP_WIDTH), lambda g, t: (g, t, COL_POOL // GROUP_WIDTH)),
            pl.BlockSpec((nseq, POOL_CARRY, GROUP_WIDTH), lambda g, t: (g, 0, 0)),
            pl.BlockSpec((n_win, POOL_GROUP, POOL_GROUP), lambda g, t: (0, 0, 0)),
            pl.BlockSpec((1, GROUP_WIDTH), lambda g, t: (0, 0)),
        ],
        out_specs=[
            pl.BlockSpec((nseq, tile, GROUP_WIDTH), lambda g, t: (g, t, 0)),
            pl.BlockSpec((nseq, POOL_HIST, GROUP_WIDTH), lambda g, t: (g, 0, 0)),
        ],
        out_shape=[
            jax.ShapeDtypeStruct((batch, seq, GROUP_WIDTH), F32),
            jax.ShapeDtypeStruct((batch, POOL_HIST, GROUP_WIDTH), F32),
        ],
        scratch_shapes=[pltpu.VMEM((nseq, POOL_CARRY + tile, GROUP_WIDTH), F32)],
        compiler_params=pltpu.CompilerParams(dimension_semantics=("parallel", "arbitrary"),
                                             vmem_limit_bytes=VMEM_LIMIT),
        name="pool_mixer",
    )(proj, hist_p, pool_w, pool_scale.reshape(1, GROUP_WIDTH))


def _reorder_w_in(w_in):
    seg = lambda start, size: w_in[..., start:start + size]
    pad = jnp.zeros(w_in.shape[:-1] + (IN_COLS_PAD - COL_SMALL - 32,), w_in.dtype)
    return jnp.concatenate([
        seg(SRC_QKV, 3072), seg(SRC_GZ, 1024), seg(SRC_SZ, 1024), seg(SRC_POOL, 1024), seg(SRC_GLU, 2048),
        seg(SRC_XBC, 1536), seg(SRC_GB, 8), seg(SRC_GA, 8), seg(SRC_DT, 16), pad], axis=-1).astype(BF16)


def kernel(x_prompt, x_sample, state_gdn, state_gdn_conv, state_ssm, state_ssm_conv, state_conv, state_pool,
           norm_mix, w_in, gdn_conv_w, gdn_a_log, gdn_dt_bias, gdn_norm_w,
           ssm_conv_w, ssm_conv_b, ssm_a_log, ssm_dt_bias, ssm_d, ssm_norm_w,
           conf_dw_w, conf_dw_b, conf_ln_w, conf_ln_b, pool_w, pool_scale,
           w_out, norm_ffn, w_gate, w_up, w_down, norm_final):
    nb, seq, d = x_prompt.shape
    db, dseq, _ = x_sample.shape
    rows_p = nb * seq
    rows_s = db * dseq

    w_in_b = _reorder_w_in(w_in)
    w_out_b = w_out.astype(BF16)
    w_down_b = w_down.astype(BF16)
    pool_w_b = pool_w.astype(BF16)

    x = jnp.concatenate([x_prompt.reshape(rows_p, d), x_sample.reshape(rows_s, d)], axis=0)

    paths = ((0, nb, seq, 64, nb, 128, 1, 0),
             (rows_p, db, dseq, dseq, 8, dseq, 8, PAST_LEN))
    sample_states = (state_gdn, state_gdn_conv, state_ssm, state_ssm_conv, state_conv, state_pool)
    new_states = [[[] for _ in sample_states] for _ in paths]

    for l in range(DEPTH):
        h = _rmsnorm(x, norm_mix[l], BF16)
        outs = []
        for pi, (row0, batch, slen, chunk, rseq, tile, cseq, pos0) in enumerate(paths):
            proj = _matmul(h, w_in_b[l], tm=1024, tn=768, name="in_proj", row0=row0, rows=batch * slen)
            proj = proj.reshape(batch, slen, IN_COLS_PAD)
            if pi == 0:
                st = [jnp.zeros((batch,) + s.shape[2:], s.dtype) for s in sample_states]
            else:
                st = [s[l] for s in sample_states]
            o_gdn, n_gdn_conv, n_gdn = _gdn(proj, st[1], st[0], gdn_conv_w[l], gdn_a_log[l], gdn_dt_bias[l],
                                            gdn_norm_w[l], chunk=chunk, nseq=rseq)
            o_ssm, n_ssm_conv, n_ssm = _ssm(proj, st[3], st[2], ssm_conv_w[l], ssm_conv_b[l], ssm_a_log[l],
                                            ssm_dt_bias[l], ssm_d[l], ssm_norm_w[l], chunk=chunk, nseq=rseq)
            o_conv, n_conv = _conf(proj, st[4], conf_dw_w[l], conf_dw_b[l], conf_ln_w[l], conf_ln_b[l],
                                   tile=tile, nseq=cseq)
            o_pool, n_pool = _pool(proj, st[5], pool_w_b[l], pool_scale[l], tile=tile, nseq=cseq, pos0=pos0)
            outs.append(jnp.concatenate([o_gdn, o_ssm, o_conv, o_pool], axis=-1).reshape(batch * slen, d))
            for i, s in enumerate((n_gdn, n_gdn_conv, n_ssm, n_ssm_conv, n_conv, n_pool)):
                new_states[pi][i].append(s)
        mix = jnp.concatenate(outs, axis=0).astype(BF16)
        x = _matmul(mix, w_out_b[l], x, tm=1024, tn=1024, name="out_proj")
        h = _rmsnorm(x, norm_ffn[l], BF16)
        ff = _gate_up(h, w_gate[l], w_up[l], tm=1024, tn=256)
        x = _matmul(ff, w_down_b[l], x, tm=768, tn=256, name="ffn_down")

    y_prompt = _rmsnorm(x, norm_final, F32, row0=0, rows=rows_p).reshape(nb, seq, d)
    y_sample = _rmsnorm(x, norm_final, F32, row0=rows_p, rows=rows_s).reshape(db, dseq, d)
    stacked = [tuple(jnp.stack(c, axis=0) for c in new_states[pi]) for pi in range(len(paths))]
    return (y_prompt, y_sample) + stacked[0] + stacked[1]
```

```python
import functools

import jax
import jax.numpy as jnp
from jax import lax
from jax.experimental import pallas as pl
from jax.experimental.pallas import tpu as pltpu

F32 = jnp.float32
BF16 = jnp.bfloat16

D_MODEL = 4096
DEPTH = 4
PAST_LEN = 16384
GROUP_WIDTH = 1024
N_MIXERS = 4
GDN_HEADS = 8
GDN_HEAD_DIM = 128
SHORT_CONV = 4
SSM_HEADS = 16
SSM_HEAD_DIM = 64
SSM_GROUPS = 2
SSM_STATE = 128
SSM_CONV_DIM = 1536
CONF_KERNEL = 31
POOL_WINDOWS = (2, 4, 8, 16)
POOL_GROUP = 256
POOL_HIST = 15
D_FF = 11008
EPS = 1e-6

SUBLANES = 8
LANES = 128
BF16_ROWS = 16
MXU_DIM = 256
VMEM_LIMIT = 56 * 1024 * 1024

COL_QKV = 0
COL_GZ = 3072
COL_SZ = 4096
COL_POOL = 5120
COL_GLU_A = 6144
COL_GLU_G = 7168
COL_X = 8192
COL_BC = 9216
COL_SMALL = 9728
LANE_GB = 0
LANE_GA = 8
LANE_DT = 16
IN_COLS_PAD = 9984

SRC_QKV, SRC_GZ, SRC_GB, SRC_GA, SRC_SZ, SRC_XBC, SRC_DT, SRC_GLU, SRC_POOL = (
    0, 3072, 4096, 4104, 4112, 5136, 6672, 6688, 8736)

CONV_CARRY = 8
CONF_CARRY = 32
POOL_CARRY = 16


def _mxu_operands(a, b):
    if a.shape[0] % BF16_ROWS == 0 and b.shape[0] % BF16_ROWS == 0:
        return a.astype(BF16), b.astype(BF16)
    return a.astype(F32), b.astype(F32)


def _dot(a, b):
    a, b = _mxu_operands(a, b)
    return jnp.dot(a, b, preferred_element_type=F32)


def _dot_nt(a, b):
    a, b = _mxu_operands(a, b)
    return lax.dot_general(a, b, (((1,), (1,)), ((), ())), preferred_element_type=F32)


def _dot_tn(a, b):
    a, b = _mxu_operands(a, b)
    return lax.dot_general(a, b, (((0,), (0,)), ((), ())), preferred_element_type=F32)


def _rows_dot_exact(sel, x):
    hi = x.astype(BF16)
    r = x - hi.astype(F32)
    mid = r.astype(BF16)
    lo = (r - mid.astype(F32)).astype(BF16)
    dot = lambda v: jnp.dot(sel, v, preferred_element_type=F32)
    return dot(hi) + (dot(mid) + dot(lo))


def _block_masks(n, shift):
    r = lax.broadcasted_iota(jnp.int32, (n, n), 0)
    q = lax.broadcasted_iota(jnp.int32, (n, n), 1)
    same = (r >> shift) == (q >> shift)
    return same, jnp.logical_and(same, r >= q), jnp.logical_and(same, r > q)


def _silu(x):
    return x * jax.nn.sigmoid(x)


def _softplus(x):
    return jnp.maximum(x, 0.0) + jnp.log1p(jnp.exp(-jnp.abs(x)))


def _layer_spec(block, layer, index_map):
    return pl.BlockSpec((None,) + tuple(block), lambda *idx: (layer,) + tuple(index_map(*idx)))


def _alias_previous(prev, n_inputs, first_output):
    if prev is None:
        return [], [], {}
    specs = [pl.BlockSpec(memory_space=pl.ANY)] * len(prev)
    return list(prev), specs, {n_inputs + k: first_output + k for k in range(len(prev))}


def _rmsnorm_kernel(x_ref, w_ref, o_ref):
    x = x_ref[...]
    ms = jnp.mean(x * x, axis=-1, keepdims=True)
    o_ref[...] = (x * lax.rsqrt(ms + EPS) * w_ref[...]).astype(o_ref.dtype)


def _rmsnorm(x, w, out_dtype, *, row0=0, rows=None, tm=256):
    t, d = x.shape
    rows = t if rows is None else rows
    blk0 = row0 // tm
    return pl.pallas_call(
        _rmsnorm_kernel,
        grid=(rows // tm,),
        in_specs=[pl.BlockSpec((tm, d), lambda i: (blk0 + i, 0)),
                  pl.BlockSpec((1, d), lambda i: (0, 0))],
        out_specs=pl.BlockSpec((tm, d), lambda i: (i, 0)),
        out_shape=jax.ShapeDtypeStruct((rows, d), out_dtype),
        compiler_params=pltpu.CompilerParams(dimension_semantics=("parallel",),
                                             vmem_limit_bytes=VMEM_LIMIT),
        name="rmsnorm",
    )(x, w.reshape(1, d))


def _matmul_kernel(a_ref, b_ref, *rest, has_res):
    part = jnp.dot(a_ref[...], b_ref[...], preferred_element_type=F32)
    if has_res:
        r_ref, o_ref = rest
        o_ref[...] = part + r_ref[...]
    else:
        rest[0][...] = part


def _matmul(a, b, res=None, *, layer, tm, tn, name, row0=0, rows=None):
    kdim = a.shape[1]
    rows = a.shape[0] if rows is None else rows
    n = b.shape[2]
    blk0 = row0 // tm
    in_specs = [pl.BlockSpec((tm, kdim), lambda i, j: (blk0 + i, 0)),
                _layer_spec((kdim, tn), layer, lambda i, j: (0, j))]
    args = [a, b]
    if res is not None:
        in_specs.append(pl.BlockSpec((tm, tn), lambda i, j: (i, j)))
        args.append(res)
    return pl.pallas_call(
        functools.partial(_matmul_kernel, has_res=res is not None),
        grid=(rows // tm, n // tn),
        in_specs=in_specs,
        out_specs=pl.BlockSpec((tm, tn), lambda i, j: (i, j)),
        out_shape=jax.ShapeDtypeStruct((rows, n), F32),
        compiler_params=pltpu.CompilerParams(dimension_semantics=("parallel", "parallel"),
                                             vmem_limit_bytes=VMEM_LIMIT),
        name=name,
    )(*args)


def _out_proj_kernel(*refs):
    part_refs = refs[:N_MIXERS]
    w_ref, x_ref, o_ref, mix_ref = refs[N_MIXERS:]

    @pl.when(pl.program_id(1) == 0)
    def _():
        for k, p_ref in enumerate(part_refs):
            mix_ref[:, k * GROUP_WIDTH:(k + 1) * GROUP_WIDTH] = p_ref[...].astype(BF16)

    o_ref[...] = jnp.dot(mix_ref[...], w_ref[...], preferred_element_type=F32) + x_ref[...]


def _out_proj(parts, w, x, *, layer, row0, tm, tn):
    rows = parts[0].shape[0]
    d = x.shape[1]
    blk0 = row0 // tm
    x_spec = pl.BlockSpec((tm, tn), lambda i, j: (blk0 + i, j))
    return pl.pallas_call(
        _out_proj_kernel,
        grid=(rows // tm, d // tn),
        in_specs=[pl.BlockSpec((tm, GROUP_WIDTH), lambda i, j: (i, 0))] * N_MIXERS
        + [_layer_spec((N_MIXERS * GROUP_WIDTH, tn), layer, lambda i, j: (0, j)), x_spec],
        out_specs=x_spec,
        out_shape=jax.ShapeDtypeStruct(x.shape, F32),
        scratch_shapes=[pltpu.VMEM((tm, N_MIXERS * GROUP_WIDTH), BF16)],
        input_output_aliases={N_MIXERS + 1: 0},
        compiler_params=pltpu.CompilerParams(dimension_semantics=("parallel", "arbitrary"),
                                             vmem_limit_bytes=VMEM_LIMIT),
        name="out_proj",
    )(*parts, w, x)


def _gate_up_kernel(h_ref, wg_ref, wu_ref, o_ref, wg_bf_ref, wu_bf_ref):
    @pl.when(pl.program_id(1) == 0)
    def _():
        wg_bf_ref[...] = wg_ref[...].astype(BF16)
        wu_bf_ref[...] = wu_ref[...].astype(BF16)

    h = h_ref[...]
    g = jnp.dot(h, wg_bf_ref[...], preferred_element_type=F32)
    u = jnp.dot(h, wu_bf_ref[...], preferred_element_type=F32)
    o_ref[...] = (_silu(g) * u).astype(o_ref.dtype)


def _gate_up(h, wg, wu, *, layer, tm, tn):
    m, d = h.shape
    n = wg.shape[2]
    w_spec = _layer_spec((d, tn), layer, lambda j, i: (0, j))
    return pl.pallas_call(
        _gate_up_kernel,
        grid=(n // tn, m // tm),
        in_specs=[pl.BlockSpec((tm, d), lambda j, i: (i, 0)), w_spec, w_spec],
        out_specs=pl.BlockSpec((tm, tn), lambda j, i: (i, j)),
        out_shape=jax.ShapeDtypeStruct((m, n), BF16),
        scratch_shapes=[pltpu.VMEM((d, tn), BF16), pltpu.VMEM((d, tn), BF16)],
        compiler_params=pltpu.CompilerParams(dimension_semantics=("parallel", "arbitrary"),
                                             vmem_limit_bytes=VMEM_LIMIT),
        name="ffn_gate_up",
    )(h, wg, wu)


def _gdn_kernel(*refs, chunk, nseq, n_tiles, hpg, zero_init, n_alias):
    qkv_ref, z_ref, sm_ref = refs[:3]
    n_state = 0 if zero_init else 2
    cw_ref, gv_ref, nw_ref = refs[3 + n_state:6 + n_state]
    o_ref, hist_out_ref, s_ref, ext_ref = refs[6 + n_state + n_alias:]
    c = chunk
    rows = nseq * c
    gr = hpg * rows
    shift = c.bit_length() - 1
    t = pl.program_id(1)

    @pl.when(t == 0)
    def _():
        ext_ref[:, 0:CONV_CARRY, :] = jnp.zeros((nseq, CONV_CARRY, ext_ref.shape[2]), F32)
        if zero_init:
            s_ref[...] = jnp.zeros(s_ref.shape, F32)
        else:
            ext_ref[:, CONV_CARRY - (SHORT_CONV - 1):CONV_CARRY, :] = refs[3][...]
            s_ref[...] = refs[4][...]

    ext_ref[:, CONV_CARRY:CONV_CARRY + c, :] = qkv_ref[...]

    same_t, tril_t, _ = _block_masks(rows, shift)
    _, tril_g, strict_g = _block_masks(gr, shift)

    sm = sm_ref[...].reshape(rows, LANES)
    beta_arr = jax.nn.sigmoid(sm)
    g_arr = -jnp.exp(gv_ref[0:1, :]) * _softplus(sm + gv_ref[1:2, :])
    gc_arr = _rows_dot_exact(jnp.where(tril_t, 1.0, 0.0).astype(BF16), g_arr)
    gl_arr = _rows_dot_exact(jnp.where(same_t, 1.0, 0.0).astype(BF16), g_arr)

    def conv(s, col0):
        acc = None
        for j in range(SHORT_CONV):
            r0 = CONV_CARRY - (SHORT_CONV - 1) + j
            term = ext_ref[s, r0:r0 + c, col0:col0 + LANES] * cw_ref[j:j + 1, col0:col0 + LANES]
            acc = term if acc is None else acc + term
        return acc

    for grp in range(GDN_HEADS // hpg):
        heads = range(grp * hpg, (grp + 1) * hpg)
        pairs = [(h, s) for h in heads for s in range(nseq)]
        stack = lambda fn: jnp.concatenate([fn(h, s) for h, s in pairs], axis=0)
        column = lambda arr, lane0: jnp.concatenate(
            [arr[:, lane0 + h:lane0 + h + 1] for h in heads], axis=0)

        qh = _silu(stack(lambda h, s: conv(s, h * GDN_HEAD_DIM)))
        kh = _silu(stack(lambda h, s: conv(s, GROUP_WIDTH + h * GDN_HEAD_DIM)))
        vh = _silu(stack(lambda h, s: conv(s, 2 * GROUP_WIDTH + h * GDN_HEAD_DIM)))
        zs = stack(lambda h, s: z_ref[s, :, h * GDN_HEAD_DIM:(h + 1) * GDN_HEAD_DIM])
        qn = qh * lax.rsqrt(jnp.sum(qh * qh, axis=-1, keepdims=True) + EPS) * (GDN_HEAD_DIM ** -0.5)
        kn = kh * lax.rsqrt(jnp.sum(kh * kh, axis=-1, keepdims=True) + EPS)
        beta = column(beta_arr, LANE_GB)
        gcol = column(gc_arr, LANE_GA)
        glast = column(gl_arr, LANE_GA)
        grow = jnp.broadcast_to(gcol, (gr, LANES)).T[0:1, :]
        decay = jnp.where(tril_g, jnp.exp(jnp.where(tril_g, gcol - grow, 0.0)), 0.0)
        kb = kn * beta
        vb = vh * beta
        lm = jnp.where(strict_g, _dot_nt(kb, kn) * decay, 0.0)

        r = -lm
        p = r
        for _ in range(shift - 1):
            p = _dot(p, p)
            r = r + p + _dot(r, p)

        egc = jnp.exp(gcol)
        rhs = jnp.concatenate([vb, kb * egc], axis=1)
        sol = rhs + _dot(r, rhs)
        u = sol[:, :GDN_HEAD_DIM]
        w = sol[:, GDN_HEAD_DIM:]
        attn = jnp.where(tril_g, _dot_nt(qn, kn) * decay, 0.0)
        qe = qn * egc
        kdec = kn * jnp.exp(glast - gcol)

        v_parts, o_parts = [], []
        for i, (h, s) in enumerate(pairs):
            st = s_ref[s, h]
            v_parts.append(u[i * c:(i + 1) * c] - _dot(w[i * c:(i + 1) * c], st))
            o_parts.append(_dot(qe[i * c:(i + 1) * c], st))
        v_new = jnp.concatenate(v_parts, axis=0)
        o = jnp.concatenate(o_parts, axis=0) + _dot(attn, v_new)
        for i, (h, s) in enumerate(pairs):
            gl = gl_arr[s * c:s * c + 1, LANE_GA + h:LANE_GA + h + 1]
            s_ref[s, h] = (s_ref[s, h] * jnp.exp(gl)
                           + _dot_tn(kdec[i * c:(i + 1) * c], v_new[i * c:(i + 1) * c]))

        on = o * lax.rsqrt(jnp.mean(o * o, axis=-1, keepdims=True) + EPS) * nw_ref[...]
        res = on * _silu(zs)
        for i, (h, s) in enumerate(pairs):
            o_ref[s, :, h * GDN_HEAD_DIM:(h + 1) * GDN_HEAD_DIM] = res[i * c:(i + 1) * c]

    @pl.when(t == n_tiles - 1)
    def _():
        hist_out_ref[...] = ext_ref[:, CONV_CARRY + c - (SHORT_CONV - 1):CONV_CARRY + c, :]

    if n_tiles > 1:
        ext_ref[:, 0:CONV_CARRY, :] = ext_ref[:, c:c + CONV_CARRY, :]


def _gdn(proj, states, prev, conv_w, a_log, dt_bias, norm_w, *, layer, chunk, nseq):
    batch, seq, _ = proj.shape
    n_tiles = seq // chunk
    hpg = MXU_DIM // (nseq * chunk)
    w3 = 3 * GROUP_WIDTH
    khist = SHORT_CONV - 1
    gv = jnp.zeros((SUBLANES, LANES), F32)
    gv = gv.at[0, LANE_GA:LANE_GA + GDN_HEADS].set(a_log).at[1, LANE_GA:LANE_GA + GDN_HEADS].set(dt_bias)
    hist_spec = _layer_spec((nseq, khist, w3), layer, lambda g, t: (g, 0, 0))
    s_spec = _layer_spec((nseq, GDN_HEADS, GDN_HEAD_DIM, GDN_HEAD_DIM), layer, lambda g, t: (g, 0, 0, 0))
    in_specs = [
        pl.BlockSpec((nseq, chunk, w3), lambda g, t: (g, t, COL_QKV // w3)),
        pl.BlockSpec((nseq, chunk, GROUP_WIDTH), lambda g, t: (g, t, COL_GZ // GROUP_WIDTH)),
        pl.BlockSpec((nseq, chunk, LANES), lambda g, t: (g, t, COL_SMALL // LANES)),
    ]
    args = [proj, proj, proj]
    if states is not None:
        in_specs += [hist_spec, s_spec]
        args += list(states)
    in_specs += [pl.BlockSpec((SHORT_CONV, w3), lambda g, t: (0, 0)),
                 pl.BlockSpec((SUBLANES, LANES), lambda g, t: (0, 0)),
                 pl.BlockSpec((1, GDN_HEAD_DIM), lambda g, t: (0, 0))]
    args += [conv_w, gv, norm_w.reshape(1, GDN_HEAD_DIM)]
    alias_args, alias_specs, aliases = _alias_previous(prev, len(args), 1)
    out, hist_new, s_new = pl.pallas_call(
        functools.partial(_gdn_kernel, chunk=chunk, nseq=nseq, n_tiles=n_tiles, hpg=hpg,
                          zero_init=states is None, n_alias=len(alias_args)),
        grid=(batch // nseq, n_tiles),
        in_specs=in_specs + alias_specs,
        out_specs=[pl.BlockSpec((nseq, chunk, GROUP_WIDTH), lambda g, t: (g, t, 0)), hist_spec, s_spec],
        out_shape=[
            jax.ShapeDtypeStruct((batch, seq, GROUP_WIDTH), F32),
            jax.ShapeDtypeStruct((DEPTH, batch, khist, w3), F32),
            jax.ShapeDtypeStruct((DEPTH, batch, GDN_HEADS, GDN_HEAD_DIM, GDN_HEAD_DIM), F32),
        ],
        scratch_shapes=[pltpu.VMEM((nseq, CONV_CARRY + chunk, w3), F32)],
        input_output_aliases=aliases,
        compiler_params=pltpu.CompilerParams(dimension_semantics=("parallel", "arbitrary"),
                                             vmem_limit_bytes=VMEM_LIMIT),
        name="gdn_mixer",
    )(*args, *alias_args)
    return out, (hist_new, s_new)


def _ssm_kernel(*refs, chunk, nseq, n_tiles, zero_init, n_alias):
    z_ref, x_ref, bc_ref, sm_ref = refs[:4]
    n_state = 0 if zero_init else 2
    cw_ref, cb_ref, sv_ref, dskip_ref, nw_ref = refs[4 + n_state:9 + n_state]
    o_ref, hist_out_ref, h_ref, ext_ref = refs[9 + n_state + n_alias:]
    c = chunk
    t = pl.program_id(1)
    n_pairs = SSM_HEADS // 2
    pairs_per_group = n_pairs // SSM_GROUPS
    x_blocks = GROUP_WIDTH // LANES
    group_w = GROUP_WIDTH // SSM_GROUPS

    @pl.when(t == 0)
    def _():
        ext_ref[:, 0:CONV_CARRY, :] = jnp.zeros((nseq, CONV_CARRY, SSM_CONV_DIM), F32)
        if zero_init:
            h_ref[...] = jnp.zeros(h_ref.shape, F32)
        else:
            ext_ref[:, CONV_CARRY - (SHORT_CONV - 1):CONV_CARRY, :] = refs[4][...]
            h_ref[...] = refs[5][...]

    ext_ref[:, CONV_CARRY:CONV_CARRY + c, 0:GROUP_WIDTH] = x_ref[...]
    ext_ref[:, CONV_CARRY:CONV_CARRY + c, GROUP_WIDTH:SSM_CONV_DIM] = bc_ref[...]

    rows = lax.broadcasted_iota(jnp.int32, (c, c), 0)
    cols = lax.broadcasted_iota(jnp.int32, (c, c), 1)
    causal = rows >= cols
    tril_b = jnp.where(causal, 1.0, 0.0).astype(BF16)
    lane_lo = lax.broadcasted_iota(jnp.int32, (c, LANES), 1) < SSM_HEAD_DIM
    row_lo = lax.broadcasted_iota(jnp.int32, (2 * SSM_HEAD_DIM, SSM_STATE), 0) < SSM_HEAD_DIM
    neg_a = -jnp.exp(sv_ref[0:1, :])

    for s in range(nseq):
        sm = sm_ref[s]
        dt_arr = _softplus(sm + sv_ref[1:2, :])
        acum = _rows_dot_exact(tril_b, dt_arr * neg_a)
        acum_t = acum.T
        dt_t = dt_arr.T

        def conv_act(blk):
            col0 = blk * LANES
            acc = cb_ref[:, col0:col0 + LANES]
            for j in range(SHORT_CONV):
                r0 = CONV_CARRY - (SHORT_CONV - 1) + j
                acc = acc + ext_ref[s, r0:r0 + c, col0:col0 + LANES] * cw_ref[j:j + 1, col0:col0 + LANES]
            return _silu(acc)

        bm = [conv_act(x_blocks + g) for g in range(SSM_GROUPS)]
        cm = [conv_act(x_blocks + SSM_GROUPS + g) for g in range(SSM_GROUPS)]
        cbm = [_dot_nt(cm[g], bm[g]) for g in range(SSM_GROUPS)]

        ys = []
        for p in range(n_pairs):
            g = p // pairs_per_group
            xp = conv_act(p)
            yd, cs, ecol, cdec = [], [], [], []
            for hh in range(2):
                lane = LANE_DT + 2 * p + hh
                col = acum[:, lane:lane + 1]
                row = acum_t[lane:lane + 1, :]
                dtrow = dt_t[lane:lane + 1, :]
                dtcol = dt_arr[:, lane:lane + 1]
                alast = acum[c - 1:c, lane:lane + 1]
                lmat = jnp.where(causal, jnp.exp(jnp.where(causal, col - row, 0.0)), 0.0)
                yd.append(_dot(cbm[g] * lmat * dtrow, xp))
                cs.append(_dot_tn(xp * (jnp.exp(alast - col) * dtcol), bm[g]))
                ecol.append(jnp.exp(col))
                cdec.append(jnp.exp(alast))
            hp = h_ref[s, p]
            y = jnp.where(lane_lo, yd[0], yd[1])
            y = y + _dot_nt(cm[g], hp) * jnp.where(lane_lo, ecol[0], ecol[1])
            h_ref[s, p] = hp * jnp.where(row_lo, cdec[0], cdec[1]) + jnp.where(row_lo, cs[0], cs[1])
            y = y + dskip_ref[:, p * LANES:(p + 1) * LANES] * xp
            ys.append(y * _silu(z_ref[s, :, p * LANES:(p + 1) * LANES]))

        for g in range(SSM_GROUPS):
            blocks = ys[g * pairs_per_group:(g + 1) * pairs_per_group]
            ssq = None
            for y in blocks:
                sq = jnp.sum(y * y, axis=-1, keepdims=True)
                ssq = sq if ssq is None else ssq + sq
            scale = lax.rsqrt(ssq / group_w + EPS)
            for i, y in enumerate(blocks):
                col0 = (g * pairs_per_group + i) * LANES
                o_ref[s, :, col0:col0 + LANES] = y * scale * nw_ref[:, col0:col0 + LANES]

    @pl.when(t == n_tiles - 1)
    def _():
        hist_out_ref[...] = ext_ref[:, CONV_CARRY + c - (SHORT_CONV - 1):CONV_CARRY + c, :]

    if n_tiles > 1:
        ext_ref[:, 0:CONV_CARRY, :] = ext_ref[:, c:c + CONV_CARRY, :]


def _ssm(proj, states, prev, conv_w, conv_b, a_log, dt_bias, d_skip, norm_w, *, layer, chunk, nseq):
    batch, seq, _ = proj.shape
    n_tiles = seq // chunk
    n_pairs = SSM_HEADS // 2
    bc_w = SSM_CONV_DIM - GROUP_WIDTH
    khist = SHORT_CONV - 1
    sv = jnp.zeros((SUBLANES, LANES), F32)
    sv = sv.at[0, LANE_DT:LANE_DT + SSM_HEADS].set(a_log).at[1, LANE_DT:LANE_DT + SSM_HEADS].set(dt_bias)
    hist_spec = _layer_spec((nseq, khist, SSM_CONV_DIM), layer, lambda g, t: (g, 0, 0))
    h_spec = _layer_spec((nseq, n_pairs, 2 * SSM_HEAD_DIM, SSM_STATE), layer, lambda g, t: (g, 0, 0, 0))
    vec_spec = lambda width: pl.BlockSpec((1, width), lambda g, t: (0, 0))
    in_specs = [
        pl.BlockSpec((nseq, chunk, GROUP_WIDTH), lambda g, t: (g, t, COL_SZ // GROUP_WIDTH)),
        pl.BlockSpec((nseq, chunk, GROUP_WIDTH), lambda g, t: (g, t, COL_X // GROUP_WIDTH)),
        pl.BlockSpec((nseq, chunk, bc_w), lambda g, t: (g, t, COL_BC // bc_w)),
        pl.BlockSpec((nseq, chunk, LANES), lambda g, t: (g, t, COL_SMALL // LANES)),
    ]
    args = [proj, proj, proj, proj]
    if states is not None:
        in_specs += [hist_spec, h_spec]
        args += list(states)
    in_specs += [pl.BlockSpec((SHORT_CONV, SSM_CONV_DIM), lambda g, t: (0, 0)), vec_spec(SSM_CONV_DIM),
                 pl.BlockSpec((SUBLANES, LANES), lambda g, t: (0, 0)), vec_spec(GROUP_WIDTH),
                 vec_spec(GROUP_WIDTH)]
    args += [conv_w, conv_b.reshape(1, SSM_CONV_DIM), sv,
             jnp.repeat(d_skip, SSM_HEAD_DIM).reshape(1, GROUP_WIDTH), norm_w.reshape(1, GROUP_WIDTH)]
    alias_args, alias_specs, aliases = _alias_previous(prev, len(args), 1)
    out, hist_new, h_new = pl.pallas_call(
        functools.partial(_ssm_kernel, chunk=chunk, nseq=nseq, n_tiles=n_tiles,
                          zero_init=states is None, n_alias=len(alias_args)),
        grid=(batch // nseq, n_tiles),
        in_specs=in_specs + alias_specs,
        out_specs=[pl.BlockSpec((nseq, chunk, GROUP_WIDTH), lambda g, t: (g, t, 0)), hist_spec, h_spec],
        out_shape=[
            jax.ShapeDtypeStruct((batch, seq, GROUP_WIDTH), F32),
            jax.ShapeDtypeStruct((DEPTH, batch, khist, SSM_CONV_DIM), F32),
            jax.ShapeDtypeStruct((DEPTH, batch, n_pairs, 2 * SSM_HEAD_DIM, SSM_STATE), F32),
        ],
        scratch_shapes=[pltpu.VMEM((nseq, CONV_CARRY + chunk, SSM_CONV_DIM), F32)],
        input_output_aliases=aliases,
        compiler_params=pltpu.CompilerParams(dimension_semantics=("parallel", "arbitrary"),
                                             vmem_limit_bytes=VMEM_LIMIT),
        name="ssm_mixer",
    )(*args, *alias_args)
    return out, (hist_new, h_new)


def _conf_kernel(*refs, tile, sub, nseq, n_tiles, zero_init, n_alias):
    a_ref, g_ref = refs[:2]
    n_state = 0 if zero_init else 1
    w_ref, b_ref, lnw_ref, lnb_ref = refs[2 + n_state:6 + n_state]
    o_ref, hist_out_ref, ext_ref = refs[6 + n_state + n_alias:]
    t = pl.program_id(1)
    khist = CONF_KERNEL - 1
    tap0 = CONF_CARRY - khist

    @pl.when(t == 0)
    def _():
        ext_ref[:, 0:CONF_CARRY, :] = jnp.zeros((nseq, CONF_CARRY, GROUP_WIDTH), F32)
        if not zero_init:
            ext_ref[:, tap0:CONF_CARRY, :] = refs[2][...]

    ext_ref[:, CONF_CARRY:CONF_CARRY + tile, :] = a_ref[...] * jax.nn.sigmoid(g_ref[...])

    for s in range(nseq):
        for cb in range(GROUP_WIDTH // LANES):
            col0 = cb * LANES
            for rb in range(tile // sub):
                base = rb * sub
                out = b_ref[:, col0:col0 + LANES]
                for res in range(SUBLANES):
                    nrows = sub if res == 0 else sub + SUBLANES
                    acc = None
                    for j in range(CONF_KERNEL):
                        if (tap0 + j) % SUBLANES != res:
                            continue
                        a0 = base + tap0 + j - res
                        term = ext_ref[s, a0:a0 + nrows, col0:col0 + LANES] * w_ref[j:j + 1, col0:col0 + LANES]
                        acc = term if acc is None else acc + term
                    if acc is not None:
                        out = out + acc[res:res + sub]
                o_ref[s, base:base + sub, col0:col0 + LANES] = out

    cv = o_ref[...]
    mu = jnp.mean(cv, axis=-1, keepdims=True)
    xc = cv - mu
    y = xc * lax.rsqrt(jnp.mean(xc * xc, axis=-1, keepdims=True) + EPS) * lnw_ref[...] + lnb_ref[...]
    o_ref[...] = _silu(y)

    @pl.when(t == n_tiles - 1)
    def _():
        hist_out_ref[...] = ext_ref[:, CONF_CARRY + tile - khist:CONF_CARRY + tile, :]

    if n_tiles > 1:
        ext_ref[:, 0:CONF_CARRY, :] = ext_ref[:, tile:tile + CONF_CARRY, :]


def _conf(proj, hist, prev, dw_w, dw_b, ln_w, ln_b, *, layer, tile, nseq):
    batch, seq, _ = proj.shape
    n_tiles = seq // tile
    khist = CONF_KERNEL - 1
    vec = lambda v: v.reshape(1, GROUP_WIDTH)
    vec_spec = pl.BlockSpec((1, GROUP_WIDTH), lambda g, t: (0, 0))
    hist_spec = _layer_spec((nseq, khist, GROUP_WIDTH), layer, lambda g, t: (g, 0, 0))
    in_specs = [
        pl.BlockSpec((nseq, tile, GROUP_WIDTH), lambda g, t: (g, t, COL_GLU_A // GROUP_WIDTH)),
        pl.BlockSpec((nseq, tile, GROUP_WIDTH), lambda g, t: (g, t, COL_GLU_G // GROUP_WIDTH)),
    ]
    args = [proj, proj]
    if hist is not None:
        in_specs.append(hist_spec)
        args.append(hist)
    in_specs += [pl.BlockSpec((CONF_KERNEL, GROUP_WIDTH), lambda g, t: (0, 0)), vec_spec, vec_spec, vec_spec]
    args += [dw_w, vec(dw_b), vec(ln_w), vec(ln_b)]
    alias_args, alias_specs, aliases = _alias_previous(prev, len(args), 1)
    out, hist_new = pl.pallas_call(
        functools.partial(_conf_kernel, tile=tile, sub=min(tile, 64), nseq=nseq, n_tiles=n_tiles,
                          zero_init=hist is None, n_alias=len(alias_args)),
        grid=(batch // nseq, n_tiles),
        in_specs=in_specs + alias_specs,
        out_specs=[pl.BlockSpec((nseq, tile, GROUP_WIDTH), lambda g, t: (g, t, 0)), hist_spec],
        out_shape=[
            jax.ShapeDtypeStruct((batch, seq, GROUP_WIDTH), F32),
            jax.ShapeDtypeStruct((DEPTH, batch, khist, GROUP_WIDTH), F32),
        ],
        scratch_shapes=[pltpu.VMEM((nseq, CONF_CARRY + tile, GROUP_WIDTH), F32)],
        input_output_aliases=aliases,
        compiler_params=pltpu.CompilerParams(dimension_semantics=("parallel", "arbitrary"),
                                             vmem_limit_bytes=VMEM_LIMIT),
        name="conformer_mixer",
    )(*args, *alias_args)
    return out, (hist_new,)


def _pool_kernel(*refs, tile, nseq, n_tiles, pos0, zero_init, n_alias):
    u_ref = refs[0]
    n_state = 0 if zero_init else 1
    pw_ref, ps_ref = refs[1 + n_state:3 + n_state]
    o_ref, hist_out_ref, ext_ref = refs[3 + n_state + n_alias:]
    t = pl.program_id(1)

    @pl.when(t == 0)
    def _():
        ext_ref[:, 0:POOL_CARRY, :] = jnp.zeros((nseq, POOL_CARRY, GROUP_WIDTH), F32)
        if not zero_init:
            ext_ref[:, POOL_CARRY - POOL_HIST:POOL_CARRY, :] = refs[1][...]

    ext_ref[:, POOL_CARRY:POOL_CARRY + tile, :] = u_ref[...]
    pos = (pos0 + t * tile + lax.broadcasted_iota(jnp.int32, (tile, 1), 0)).astype(F32)

    for s in range(nseq):
        for gi, win in enumerate(POOL_WINDOWS):
            col0 = gi * POOL_GROUP
            acc = None
            for i in range(win):
                term = ext_ref[s, POOL_CARRY - i:POOL_CARRY - i + tile, col0:col0 + POOL_GROUP]
                acc = term if acc is None else acc + term
            cnt = jnp.minimum(jnp.float32(win), pos + 1.0)
            pooled = acc / cnt - u_ref[s, :, col0:col0 + POOL_GROUP]
            y = _dot(pooled, pw_ref[gi])
            o_ref[s, :, col0:col0 + POOL_GROUP] = y * ps_ref[:, col0:col0 + POOL_GROUP]

    @pl.when(t == n_tiles - 1)
    def _():
        hist_out_ref[...] = ext_ref[:, POOL_CARRY + tile - POOL_HIST:POOL_CARRY + tile, :]

    if n_tiles > 1:
        ext_ref[:, 0:POOL_CARRY, :] = ext_ref[:, tile:tile + POOL_CARRY, :]


def _pool(proj, hist, prev, pool_w, pool_scale, *, layer, tile, nseq, pos0):
    batch, seq, _ = proj.shape
    n_tiles = seq // tile
    n_win = len(POOL_WINDOWS)
    hist_spec = _layer_spec((nseq, POOL_HIST, GROUP_WIDTH), layer, lambda g, t: (g, 0, 0))
    in_specs = [pl.BlockSpec((nseq, tile, GROUP_WIDTH), lambda g, t: (g, t, COL_POOL // GROUP_WIDTH))]
    args = [proj]
    if hist is not None:
        in_specs.append(hist_spec)
        args.append(hist)
    in_specs += [pl.BlockSpec((n_win, POOL_GROUP, POOL_GROUP), lambda g, t: (0, 0, 0)),
                 pl.BlockSpec((1, GROUP_WIDTH), lambda g, t: (0, 0))]
    args += [pool_w, pool_scale.reshape(1, GROUP_WIDTH)]
    alias_args, alias_specs, aliases = _alias_previous(prev, len(args), 1)
    out, hist_new = pl.pallas_call(
        functools.partial(_pool_kernel, tile=tile, nseq=nseq, n_tiles=n_tiles, pos0=pos0,
                          zero_init=hist is None, n_alias=len(alias_args)),
        grid=(batch // nseq, n_tiles),
        in_specs=in_specs + alias_specs,
        out_specs=[pl.BlockSpec((nseq, tile, GROUP_WIDTH), lambda g, t: (g, t, 0)), hist_spec],
        out_shape=[
            jax.ShapeDtypeStruct((batch, seq, GROUP_WIDTH), F32),
            jax.ShapeDtypeStruct((DEPTH, batch, POOL_HIST, GROUP_WIDTH), F32),
        ],
        scratch_shapes=[pltpu.VMEM((nseq, POOL_CARRY + tile, GROUP_WIDTH), F32)],
        input_output_aliases=aliases,
        compiler_params=pltpu.CompilerParams(dimension_semantics=("parallel", "arbitrary"),
                                             vmem_limit_bytes=VMEM_LIMIT),
        name="pool_mixer",
    )(*args, *alias_args)
    return out, (hist_new,)


def _reorder_w_in(w_in):
    seg = lambda start, size: w_in[..., start:start + size]
    pad = jnp.zeros(w_in.shape[:-1] + (IN_COLS_PAD - COL_SMALL - 32,), w_in.dtype)
    return jnp.concatenate([
        seg(SRC_QKV, 3072), seg(SRC_GZ, 1024), seg(SRC_SZ, 1024), seg(SRC_POOL, 1024), seg(SRC_GLU, 2048),
        seg(SRC_XBC, 1536), seg(SRC_GB, 8), seg(SRC_GA, 8), seg(SRC_DT, 16), pad], axis=-1).astype(BF16)


def kernel(x_prompt, x_sample, state_gdn, state_gdn_conv, state_ssm, state_ssm_conv, state_conv, state_pool,
           norm_mix, w_in, gdn_conv_w, gdn_a_log, gdn_dt_bias, gdn_norm_w,
           ssm_conv_w, ssm_conv_b, ssm_a_log, ssm_dt_bias, ssm_d, ssm_norm_w,
           conf_dw_w, conf_dw_b, conf_ln_w, conf_ln_b, pool_w, pool_scale,
           w_out, norm_ffn, w_gate, w_up, w_down, norm_final):
    nb, seq, d = x_prompt.shape
    db, dseq, _ = x_sample.shape
    rows_p = nb * seq
    rows_s = db * dseq
    n_pairs = SSM_HEADS // 2
    pair_shape = (n_pairs, 2 * SSM_HEAD_DIM, SSM_STATE)

    w_in_b = _reorder_w_in(w_in)
    w_out_b = w_out.astype(BF16)
    w_down_b = w_down.astype(BF16)
    pool_w_b = pool_w.astype(BF16)

    x = jnp.concatenate([x_prompt.reshape(rows_p, d), x_sample.reshape(rows_s, d)], axis=0)

    decode_states = dict(
        gdn=(state_gdn_conv, state_gdn),
        ssm=(state_ssm_conv, state_ssm.reshape((DEPTH, db) + pair_shape)),
        conf=state_conv, pool=state_pool)
    no_states = dict(gdn=None, ssm=None, conf=None, pool=None)
    paths = ((0, nb, seq, 64, nb, 128, 1, 0, no_states),
             (rows_p, db, dseq, dseq, 8, dseq, 8, PAST_LEN, decode_states))
    stacked = [dict(gdn=None, ssm=None, conf=None, pool=None) for _ in paths]

    for l in range(DEPTH):
        h = _rmsnorm(x, norm_mix[l], BF16)
        for pi, (row0, batch, slen, chunk, rseq, tile, cseq, pos0, st) in enumerate(paths):
            proj = _matmul(h, w_in_b, layer=l, tm=1024, tn=768, name="in_proj", row0=row0, rows=batch * slen)
            proj = proj.reshape(batch, slen, IN_COLS_PAD)
            prev = stacked[pi]
            o_gdn, prev["gdn"] = _gdn(proj, st["gdn"], prev["gdn"], gdn_conv_w[l], gdn_a_log[l],
                                      gdn_dt_bias[l], gdn_norm_w[l], layer=l, chunk=chunk, nseq=rseq)
            o_ssm, prev["ssm"] = _ssm(proj, st["ssm"], prev["ssm"], ssm_conv_w[l], ssm_conv_b[l], ssm_a_log[l],
                                      ssm_dt_bias[l], ssm_d[l], ssm_norm_w[l], layer=l, chunk=chunk, nseq=rseq)
            o_conv, prev["conf"] = _conf(proj, st["conf"], prev["conf"], conf_dw_w[l], conf_dw_b[l],
                                         conf_ln_w[l], conf_ln_b[l], layer=l, tile=tile, nseq=cseq)
            o_pool, prev["pool"] = _pool(proj, st["pool"], prev["pool"], pool_w_b[l], pool_scale[l],
                                         layer=l, tile=tile, nseq=cseq, pos0=pos0)
            parts = [o.reshape(batch * slen, GROUP_WIDTH) for o in (o_gdn, o_ssm, o_conv, o_pool)]
            x = _out_proj(parts, w_out_b, x, layer=l, row0=row0, tm=512, tn=1024)
        h = _rmsnorm(x, norm_ffn[l], BF16)
        ff = _gate_up(h, w_gate, w_up, layer=l, tm=1536, tn=256)
        x = _matmul(ff, w_down_b, x, layer=l, tm=768, tn=256, name="ffn_down")

    y_prompt = _rmsnorm(x, norm_final, F32, row0=0, rows=rows_p).reshape(nb, seq, d)
    y_sample = _rmsnorm(x, norm_final, F32, row0=rows_p, rows=rows_s).reshape(db, dseq, d)

    def states_out(st, batch):
        gdn_conv, gdn_s = st["gdn"]
        ssm_conv, ssm_h = st["ssm"]
        return (gdn_s, gdn_conv, ssm_h.reshape(DEPTH, batch, SSM_HEADS, SSM_HEAD_DIM, SSM_STATE), ssm_conv,
                st["conf"][0], st["pool"][0])

    return (y_prompt, y_sample) + states_out(stacked[0], nb) + states_out(stacked[1], db)
```

```python
import functools

import jax
import jax.numpy as jnp
from jax import lax
from jax.experimental import pallas as pl
from jax.experimental.pallas import tpu as pltpu

F32 = jnp.float32
BF16 = jnp.bfloat16

D_MODEL = 4096
DEPTH = 4
PAST_LEN = 16384
GROUP_WIDTH = 1024
N_MIXERS = 4
GDN_HEADS = 8
GDN_HEAD_DIM = 128
SHORT_CONV = 4
SSM_HEADS = 16
SSM_HEAD_DIM = 64
SSM_GROUPS = 2
SSM_STATE = 128
SSM_CONV_DIM = 1536
CONF_KERNEL = 31
POOL_WINDOWS = (2, 4, 8, 16)
POOL_GROUP = 256
POOL_HIST = 15
D_FF = 11008
EPS = 1e-6

SUBLANES = 8
LANES = 128
BF16_ROWS = 16
MXU_DIM = 256
VMEM_LIMIT = 56 * 1024 * 1024

COL_QKV = 0
COL_GZ = 3072
COL_SZ = 4096
COL_POOL = 5120
COL_GLU_A = 6144
COL_GLU_G = 7168
COL_X = 8192
COL_BC = 9216
COL_SMALL = 9728
LANE_GB = 0
LANE_GA = 8
LANE_DT = 16
IN_COLS_PAD = 9984

SRC_QKV, SRC_GZ, SRC_GB, SRC_GA, SRC_SZ, SRC_XBC, SRC_DT, SRC_GLU, SRC_POOL = (
    0, 3072, 4096, 4104, 4112, 5136, 6672, 6688, 8736)

CONV_CARRY = 8
CONF_CARRY = 32
POOL_CARRY = 16


def _mxu_operands(a, b):
    if a.shape[0] % BF16_ROWS == 0 and b.shape[0] % BF16_ROWS == 0:
        return a.astype(BF16), b.astype(BF16)
    return a.astype(F32), b.astype(F32)


def _dot(a, b):
    a, b = _mxu_operands(a, b)
    return jnp.dot(a, b, preferred_element_type=F32)


def _dot_nt(a, b):
    a, b = _mxu_operands(a, b)
    return lax.dot_general(a, b, (((1,), (1,)), ((), ())), preferred_element_type=F32)


def _dot_tn(a, b):
    a, b = _mxu_operands(a, b)
    return lax.dot_general(a, b, (((0,), (0,)), ((), ())), preferred_element_type=F32)


def _rows_dot_exact(sel, x):
    hi = x.astype(BF16)
    r = x - hi.astype(F32)
    mid = r.astype(BF16)
    lo = (r - mid.astype(F32)).astype(BF16)
    dot = lambda v: jnp.dot(sel, v, preferred_element_type=F32)
    return dot(hi) + (dot(mid) + dot(lo))


def _block_masks(n, shift):
    r = lax.broadcasted_iota(jnp.int32, (n, n), 0)
    q = lax.broadcasted_iota(jnp.int32, (n, n), 1)
    same = (r >> shift) == (q >> shift)
    return same, jnp.logical_and(same, r >= q), jnp.logical_and(same, r > q)


def _silu(x):
    return x * jax.nn.sigmoid(x)


def _softplus(x):
    return jnp.maximum(x, 0.0) + jnp.log1p(jnp.exp(-jnp.abs(x)))


def _layer_spec(block, layer, index_map):
    return pl.BlockSpec((None,) + tuple(block), lambda *idx: (layer,) + tuple(index_map(*idx)))


def _alias_previous(prev, n_inputs, first_output):
    if prev is None:
        return [], [], {}
    specs = [pl.BlockSpec(memory_space=pl.ANY)] * len(prev)
    return list(prev), specs, {n_inputs + k: first_output + k for k in range(len(prev))}


def _rmsnorm_kernel(x_ref, w_ref, o_ref):
    x = x_ref[...]
    ms = jnp.mean(x * x, axis=-1, keepdims=True)
    o_ref[...] = (x * lax.rsqrt(ms + EPS) * w_ref[...]).astype(o_ref.dtype)


def _rmsnorm(x, w, out_dtype, *, row0=0, rows=None, tm=256):
    t, d = x.shape
    rows = t if rows is None else rows
    blk0 = row0 // tm
    return pl.pallas_call(
        _rmsnorm_kernel,
        grid=(rows // tm,),
        in_specs=[pl.BlockSpec((tm, d), lambda i: (blk0 + i, 0)),
                  pl.BlockSpec((1, d), lambda i: (0, 0))],
        out_specs=pl.BlockSpec((tm, d), lambda i: (i, 0)),
        out_shape=jax.ShapeDtypeStruct((rows, d), out_dtype),
        compiler_params=pltpu.CompilerParams(dimension_semantics=("parallel",),
                                             vmem_limit_bytes=VMEM_LIMIT),
        name="rmsnorm",
    )(x, w.reshape(1, d))


def _matmul_kernel(a_ref, b_ref, *rest, has_res):
    part = jnp.dot(a_ref[...], b_ref[...], preferred_element_type=F32)
    if has_res:
        r_ref, o_ref = rest
        o_ref[...] = part + r_ref[...]
    else:
        rest[0][...] = part


def _matmul(a, b, res=None, *, layer, tm, tn, name, row0=0, rows=None):
    kdim = a.shape[1]
    rows = a.shape[0] if rows is None else rows
    n = b.shape[2]
    blk0 = row0 // tm
    in_specs = [pl.BlockSpec((tm, kdim), lambda i, j: (blk0 + i, 0)),
                _layer_spec((kdim, tn), layer, lambda i, j: (0, j))]
    args = [a, b]
    if res is not None:
        in_specs.append(pl.BlockSpec((tm, tn), lambda i, j: (i, j)))
        args.append(res)
    return pl.pallas_call(
        functools.partial(_matmul_kernel, has_res=res is not None),
        grid=(rows // tm, n // tn),
        in_specs=in_specs,
        out_specs=pl.BlockSpec((tm, tn), lambda i, j: (i, j)),
        out_shape=jax.ShapeDtypeStruct((rows, n), F32),
        compiler_params=pltpu.CompilerParams(dimension_semantics=("parallel", "parallel"),
                                             vmem_limit_bytes=VMEM_LIMIT),
        name=name,
    )(*args)


def _out_proj_kernel(*refs):
    part_refs = refs[:N_MIXERS]
    w_ref, x_ref = refs[N_MIXERS:N_MIXERS + 2]
    o_ref, mix_ref = refs[-2:]

    @pl.when(pl.program_id(1) == 0)
    def _():
        for k, p_ref in enumerate(part_refs):
            mix_ref[:, k * GROUP_WIDTH:(k + 1) * GROUP_WIDTH] = p_ref[...].astype(BF16)

    o_ref[...] = jnp.dot(mix_ref[...], w_ref[...], preferred_element_type=F32) + x_ref[...]


def _out_proj(parts, w, x, *, layer, row0, tm, tn, res=None, total_rows=None):
    rows = parts[0].shape[0]
    d = w.shape[2]
    blk0 = row0 // tm
    x_spec = pl.BlockSpec((tm, tn), lambda i, j: (blk0 + i, j))
    in_specs = ([pl.BlockSpec((tm, GROUP_WIDTH), lambda i, j: (i, 0))] * N_MIXERS
                + [_layer_spec((N_MIXERS * GROUP_WIDTH, tn), layer, lambda i, j: (0, j))])
    args = list(parts) + [w]
    if res is None:
        in_specs.append(x_spec)
        args.append(x)
        aliases = {len(args) - 1: 0}
    else:
        in_specs.append(pl.BlockSpec((tm, tn), lambda i, j: (i, j)))
        args.append(res)
        aliases = {}
        if x is not None:
            in_specs.append(pl.BlockSpec(memory_space=pl.ANY))
            args.append(x)
            aliases = {len(args) - 1: 0}
    out_rows = x.shape[0] if x is not None else total_rows
    return pl.pallas_call(
        _out_proj_kernel,
        grid=(rows // tm, d // tn),
        in_specs=in_specs,
        out_specs=x_spec,
        out_shape=jax.ShapeDtypeStruct((out_rows, d), F32),
        scratch_shapes=[pltpu.VMEM((tm, N_MIXERS * GROUP_WIDTH), BF16)],
        input_output_aliases=aliases,
        compiler_params=pltpu.CompilerParams(dimension_semantics=("parallel", "arbitrary"),
                                             vmem_limit_bytes=VMEM_LIMIT),
        name="out_proj",
    )(*args)


def _gate_up_kernel(h_ref, wg_ref, wu_ref, o_ref, wg_bf_ref, wu_bf_ref):
    @pl.when(pl.program_id(1) == 0)
    def _():
        wg_bf_ref[...] = wg_ref[...].astype(BF16)
        wu_bf_ref[...] = wu_ref[...].astype(BF16)

    h = h_ref[...]
    g = jnp.dot(h, wg_bf_ref[...], preferred_element_type=F32)
    u = jnp.dot(h, wu_bf_ref[...], preferred_element_type=F32)
    o_ref[...] = (_silu(g) * u).astype(o_ref.dtype)


def _gate_up(h, wg, wu, *, layer, tm, tn):
    m, d = h.shape
    n = wg.shape[2]
    w_spec = _layer_spec((d, tn), layer, lambda j, i: (0, j))
    return pl.pallas_call(
        _gate_up_kernel,
        grid=(n // tn, m // tm),
        in_specs=[pl.BlockSpec((tm, d), lambda j, i: (i, 0)), w_spec, w_spec],
        out_specs=pl.BlockSpec((tm, tn), lambda j, i: (i, j)),
        out_shape=jax.ShapeDtypeStruct((m, n), BF16),
        scratch_shapes=[pltpu.VMEM((d, tn), BF16), pltpu.VMEM((d, tn), BF16)],
        compiler_params=pltpu.CompilerParams(dimension_semantics=("parallel", "arbitrary"),
                                             vmem_limit_bytes=VMEM_LIMIT),
        name="ffn_gate_up",
    )(h, wg, wu)


def _gdn_kernel(*refs, chunk, nseq, n_tiles, hpg, zero_init, n_alias):
    qkv_ref, z_ref, sm_ref = refs[:3]
    n_state = 0 if zero_init else 2
    cw_ref, gv_ref, nw_ref = refs[3 + n_state:6 + n_state]
    o_ref, hist_out_ref, s_ref, ext_ref = refs[6 + n_state + n_alias:]
    c = chunk
    rows = nseq * c
    gr = hpg * rows
    shift = c.bit_length() - 1
    t = pl.program_id(1)

    @pl.when(t == 0)
    def _():
        ext_ref[:, 0:CONV_CARRY, :] = jnp.zeros((nseq, CONV_CARRY, ext_ref.shape[2]), F32)
        if zero_init:
            s_ref[...] = jnp.zeros(s_ref.shape, F32)
        else:
            ext_ref[:, CONV_CARRY - (SHORT_CONV - 1):CONV_CARRY, :] = refs[3][...]
            s_ref[...] = refs[4][...]

    ext_ref[:, CONV_CARRY:CONV_CARRY + c, :] = qkv_ref[...]

    same_t, tril_t, _ = _block_masks(rows, shift)
    _, tril_g, strict_g = _block_masks(gr, shift)

    sm = sm_ref[...].reshape(rows, LANES)
    beta_arr = jax.nn.sigmoid(sm)
    g_arr = -jnp.exp(gv_ref[0:1, :]) * _softplus(sm + gv_ref[1:2, :])
    gc_arr = _rows_dot_exact(jnp.where(tril_t, 1.0, 0.0).astype(BF16), g_arr)
    gl_arr = _rows_dot_exact(jnp.where(same_t, 1.0, 0.0).astype(BF16), g_arr)

    def conv(s, col0):
        acc = None
        for j in range(SHORT_CONV):
            r0 = CONV_CARRY - (SHORT_CONV - 1) + j
            term = ext_ref[s, r0:r0 + c, col0:col0 + LANES] * cw_ref[j:j + 1, col0:col0 + LANES]
            acc = term if acc is None else acc + term
        return acc

    for grp in range(GDN_HEADS // hpg):
        heads = range(grp * hpg, (grp + 1) * hpg)
        pairs = [(h, s) for h in heads for s in range(nseq)]
        stack = lambda fn: jnp.concatenate([fn(h, s) for h, s in pairs], axis=0)
        column = lambda arr, lane0: jnp.concatenate(
            [arr[:, lane0 + h:lane0 + h + 1] for h in heads], axis=0)

        qh = _silu(stack(lambda h, s: conv(s, h * GDN_HEAD_DIM)))
        kh = _silu(stack(lambda h, s: conv(s, GROUP_WIDTH + h * GDN_HEAD_DIM)))
        vh = _silu(stack(lambda h, s: conv(s, 2 * GROUP_WIDTH + h * GDN_HEAD_DIM)))
        zs = stack(lambda h, s: z_ref[s, :, h * GDN_HEAD_DIM:(h + 1) * GDN_HEAD_DIM])
        qn = qh * lax.rsqrt(jnp.sum(qh * qh, axis=-1, keepdims=True) + EPS) * (GDN_HEAD_DIM ** -0.5)
        kn = kh * lax.rsqrt(jnp.sum(kh * kh, axis=-1, keepdims=True) + EPS)
        beta = column(beta_arr, LANE_GB)
        gcol = column(gc_arr, LANE_GA)
        glast = column(gl_arr, LANE_GA)
        grow = jnp.broadcast_to(gcol, (gr, LANES)).T[0:1, :]
        decay = jnp.where(tril_g, jnp.exp(jnp.where(tril_g, gcol - grow, 0.0)), 0.0)
        kb = kn * beta
        vb = vh * beta
        lm = jnp.where(strict_g, _dot_nt(kb, kn) * decay, 0.0)

        r = -lm
        p = r
        for _ in range(shift - 1):
            p = _dot(p, p)
            r = r + p + _dot(r, p)

        egc = jnp.exp(gcol)
        rhs = jnp.concatenate([vb, kb * egc], axis=1)
        sol = rhs + _dot(r, rhs)
        u = sol[:, :GDN_HEAD_DIM]
        w = sol[:, GDN_HEAD_DIM:]
        attn = jnp.where(tril_g, _dot_nt(qn, kn) * decay, 0.0)
        qe = qn * egc
        kdec = kn * jnp.exp(glast - gcol)

        v_parts, o_parts = [], []
        for i, (h, s) in enumerate(pairs):
            st = s_ref[s, h]
            v_parts.append(u[i * c:(i + 1) * c] - _dot(w[i * c:(i + 1) * c], st))
            o_parts.append(_dot(qe[i * c:(i + 1) * c], st))
        v_new = jnp.concatenate(v_parts, axis=0)
        o = jnp.concatenate(o_parts, axis=0) + _dot(attn, v_new)
        for i, (h, s) in enumerate(pairs):
            gl = gl_arr[s * c:s * c + 1, LANE_GA + h:LANE_GA + h + 1]
            s_ref[s, h] = (s_ref[s, h] * jnp.exp(gl)
                           + _dot_tn(kdec[i * c:(i + 1) * c], v_new[i * c:(i + 1) * c]))

        on = o * lax.rsqrt(jnp.mean(o * o, axis=-1, keepdims=True) + EPS) * nw_ref[...]
        res = on * _silu(zs)
        for i, (h, s) in enumerate(pairs):
            o_ref[s, :, h * GDN_HEAD_DIM:(h + 1) * GDN_HEAD_DIM] = res[i * c:(i + 1) * c]

    @pl.when(t == n_tiles - 1)
    def _():
        hist_out_ref[...] = ext_ref[:, CONV_CARRY + c - (SHORT_CONV - 1):CONV_CARRY + c, :]

    if n_tiles > 1:
        ext_ref[:, 0:CONV_CARRY, :] = ext_ref[:, c:c + CONV_CARRY, :]


def _gdn(proj, states, prev, conv_w, a_log, dt_bias, norm_w, *, layer, chunk, nseq):
    batch, seq, _ = proj.shape
    n_tiles = seq // chunk
    hpg = MXU_DIM // (nseq * chunk)
    w3 = 3 * GROUP_WIDTH
    khist = SHORT_CONV - 1
    gv = jnp.zeros((SUBLANES, LANES), F32)
    gv = gv.at[0, LANE_GA:LANE_GA + GDN_HEADS].set(a_log).at[1, LANE_GA:LANE_GA + GDN_HEADS].set(dt_bias)
    hist_spec = _layer_spec((nseq, khist, w3), layer, lambda g, t: (g, 0, 0))
    s_spec = _layer_spec((nseq, GDN_HEADS, GDN_HEAD_DIM, GDN_HEAD_DIM), layer, lambda g, t: (g, 0, 0, 0))
    in_specs = [
        pl.BlockSpec((nseq, chunk, w3), lambda g, t: (g, t, COL_QKV // w3)),
        pl.BlockSpec((nseq, chunk, GROUP_WIDTH), lambda g, t: (g, t, COL_GZ // GROUP_WIDTH)),
        pl.BlockSpec((nseq, chunk, LANES), lambda g, t: (g, t, COL_SMALL // LANES)),
    ]
    args = [proj, proj, proj]
    if states is not None:
        in_specs += [hist_spec, s_spec]
        args += list(states)
    in_specs += [pl.BlockSpec((SHORT_CONV, w3), lambda g, t: (0, 0)),
                 pl.BlockSpec((SUBLANES, LANES), lambda g, t: (0, 0)),
                 pl.BlockSpec((1, GDN_HEAD_DIM), lambda g, t: (0, 0))]
    args += [conv_w, gv, norm_w.reshape(1, GDN_HEAD_DIM)]
    alias_args, alias_specs, aliases = _alias_previous(prev, len(args), 1)
    out, hist_new, s_new = pl.pallas_call(
        functools.partial(_gdn_kernel, chunk=chunk, nseq=nseq, n_tiles=n_tiles, hpg=hpg,
                          zero_init=states is None, n_alias=len(alias_args)),
        grid=(batch // nseq, n_tiles),
        in_specs=in_specs + alias_specs,
        out_specs=[pl.BlockSpec((nseq, chunk, GROUP_WIDTH), lambda g, t: (g, t, 0)), hist_spec, s_spec],
        out_shape=[
            jax.ShapeDtypeStruct((batch, seq, GROUP_WIDTH), F32),
            jax.ShapeDtypeStruct((DEPTH, batch, khist, w3), F32),
            jax.ShapeDtypeStruct((DEPTH, batch, GDN_HEADS, GDN_HEAD_DIM, GDN_HEAD_DIM), F32),
        ],
        scratch_shapes=[pltpu.VMEM((nseq, CONV_CARRY + chunk, w3), F32)],
        input_output_aliases=aliases,
        compiler_params=pltpu.CompilerParams(dimension_semantics=("parallel", "arbitrary"),
                                             vmem_limit_bytes=VMEM_LIMIT),
        name="gdn_mixer",
    )(*args, *alias_args)
    return out, (hist_new, s_new)


def _ssm_kernel(*refs, chunk, nseq, n_tiles, zero_init, n_alias):
    z_ref, x_ref, bc_ref, sm_ref = refs[:4]
    n_state = 0 if zero_init else 2
    cw_ref, cb_ref, sv_ref, dskip_ref, nw_ref = refs[4 + n_state:9 + n_state]
    o_ref, hist_out_ref, h_ref, ext_ref = refs[9 + n_state + n_alias:]
    c = chunk
    t = pl.program_id(1)
    n_pairs = SSM_HEADS // 2
    pairs_per_group = n_pairs // SSM_GROUPS
    x_blocks = GROUP_WIDTH // LANES
    group_w = GROUP_WIDTH // SSM_GROUPS

    @pl.when(t == 0)
    def _():
        ext_ref[:, 0:CONV_CARRY, :] = jnp.zeros((nseq, CONV_CARRY, SSM_CONV_DIM), F32)
        if zero_init:
            h_ref[...] = jnp.zeros(h_ref.shape, F32)
        else:
            ext_ref[:, CONV_CARRY - (SHORT_CONV - 1):CONV_CARRY, :] = refs[4][...]
            h_ref[...] = refs[5][...]

    ext_ref[:, CONV_CARRY:CONV_CARRY + c, 0:GROUP_WIDTH] = x_ref[...]
    ext_ref[:, CONV_CARRY:CONV_CARRY + c, GROUP_WIDTH:SSM_CONV_DIM] = bc_ref[...]

    rows = lax.broadcasted_iota(jnp.int32, (c, c), 0)
    cols = lax.broadcasted_iota(jnp.int32, (c, c), 1)
    causal = rows >= cols
    tril_b = jnp.where(causal, 1.0, 0.0).astype(BF16)
    lane_lo = lax.broadcasted_iota(jnp.int32, (c, LANES), 1) < SSM_HEAD_DIM
    row_lo = lax.broadcasted_iota(jnp.int32, (2 * SSM_HEAD_DIM, SSM_STATE), 0) < SSM_HEAD_DIM
    neg_a = -jnp.exp(sv_ref[0:1, :])

    for s in range(nseq):
        sm = sm_ref[s]
        dt_arr = _softplus(sm + sv_ref[1:2, :])
        acum = _rows_dot_exact(tril_b, dt_arr * neg_a)
        acum_t = acum.T
        dt_t = dt_arr.T

        def conv_act(blk):
            col0 = blk * LANES
            acc = cb_ref[:, col0:col0 + LANES]
            for j in range(SHORT_CONV):
                r0 = CONV_CARRY - (SHORT_CONV - 1) + j
                acc = acc + ext_ref[s, r0:r0 + c, col0:col0 + LANES] * cw_ref[j:j + 1, col0:col0 + LANES]
            return _silu(acc)

        bm = [conv_act(x_blocks + g) for g in range(SSM_GROUPS)]
        cm = [conv_act(x_blocks + SSM_GROUPS + g) for g in range(SSM_GROUPS)]
        cbm = [_dot_nt(cm[g], bm[g]) for g in range(SSM_GROUPS)]

        ys = []
        for p in range(n_pairs):
            g = p // pairs_per_group
            xp = conv_act(p)
            yd, cs, ecol, cdec = [], [], [], []
            for hh in range(2):
                lane = LANE_DT + 2 * p + hh
                col = acum[:, lane:lane + 1]
                row = acum_t[lane:lane + 1, :]
                dtrow = dt_t[lane:lane + 1, :]
                dtcol = dt_arr[:, lane:lane + 1]
                alast = acum[c - 1:c, lane:lane + 1]
                lmat = jnp.where(causal, jnp.exp(jnp.where(causal, col - row, 0.0)), 0.0)
                yd.append(_dot(cbm[g] * lmat * dtrow, xp))
                cs.append(_dot_tn(xp * (jnp.exp(alast - col) * dtcol), bm[g]))
                ecol.append(jnp.exp(col))
                cdec.append(jnp.exp(alast))
            hp = h_ref[s, p]
            y = jnp.where(lane_lo, yd[0], yd[1])
            y = y + _dot_nt(cm[g], hp) * jnp.where(lane_lo, ecol[0], ecol[1])
            h_ref[s, p] = hp * jnp.where(row_lo, cdec[0], cdec[1]) + jnp.where(row_lo, cs[0], cs[1])
            y = y + dskip_ref[:, p * LANES:(p + 1) * LANES] * xp
            ys.append(y * _silu(z_ref[s, :, p * LANES:(p + 1) * LANES]))

        for g in range(SSM_GROUPS):
            blocks = ys[g * pairs_per_group:(g + 1) * pairs_per_group]
            ssq = None
            for y in blocks:
                sq = jnp.sum(y * y, axis=-1, keepdims=True)
                ssq = sq if ssq is None else ssq + sq
            scale = lax.rsqrt(ssq / group_w + EPS)
            for i, y in enumerate(blocks):
                col0 = (g * pairs_per_group + i) * LANES
                o_ref[s, :, col0:col0 + LANES] = y * scale * nw_ref[:, col0:col0 + LANES]

    @pl.when(t == n_tiles - 1)
    def _():
        hist_out_ref[...] = ext_ref[:, CONV_CARRY + c - (SHORT_CONV - 1):CONV_CARRY + c, :]

    if n_tiles > 1:
        ext_ref[:, 0:CONV_CARRY, :] = ext_ref[:, c:c + CONV_CARRY, :]


def _ssm(proj, states, prev, conv_w, conv_b, a_log, dt_bias, d_skip, norm_w, *, layer, chunk, nseq):
    batch, seq, _ = proj.shape
    n_tiles = seq // chunk
    n_pairs = SSM_HEADS // 2
    bc_w = SSM_CONV_DIM - GROUP_WIDTH
    khist = SHORT_CONV - 1
    sv = jnp.zeros((SUBLANES, LANES), F32)
    sv = sv.at[0, LANE_DT:LANE_DT + SSM_HEADS].set(a_log).at[1, LANE_DT:LANE_DT + SSM_HEADS].set(dt_bias)
    hist_spec = _layer_spec((nseq, khist, SSM_CONV_DIM), layer, lambda g, t: (g, 0, 0))
    h_spec = _layer_spec((nseq, n_pairs, 2 * SSM_HEAD_DIM, SSM_STATE), layer, lambda g, t: (g, 0, 0, 0))
    vec_spec = lambda width: pl.BlockSpec((1, width), lambda g, t: (0, 0))
    in_specs = [
        pl.BlockSpec((nseq, chunk, GROUP_WIDTH), lambda g, t: (g, t, COL_SZ // GROUP_WIDTH)),
        pl.BlockSpec((nseq, chunk, GROUP_WIDTH), lambda g, t: (g, t, COL_X // GROUP_WIDTH)),
        pl.BlockSpec((nseq, chunk, bc_w), lambda g, t: (g, t, COL_BC // bc_w)),
        pl.BlockSpec((nseq, chunk, LANES), lambda g, t: (g, t, COL_SMALL // LANES)),
    ]
    args = [proj, proj, proj, proj]
    if states is not None:
        in_specs += [hist_spec, h_spec]
        args += list(states)
    in_specs += [pl.BlockSpec((SHORT_CONV, SSM_CONV_DIM), lambda g, t: (0, 0)), vec_spec(SSM_CONV_DIM),
                 pl.BlockSpec((SUBLANES, LANES), lambda g, t: (0, 0)), vec_spec(GROUP_WIDTH),
                 vec_spec(GROUP_WIDTH)]
    args += [conv_w, conv_b.reshape(1, SSM_CONV_DIM), sv,
             jnp.repeat(d_skip, SSM_HEAD_DIM).reshape(1, GROUP_WIDTH), norm_w.reshape(1, GROUP_WIDTH)]
    alias_args, alias_specs, aliases = _alias_previous(prev, len(args), 1)
    out, hist_new, h_new = pl.pallas_call(
        functools.partial(_ssm_kernel, chunk=chunk, nseq=nseq, n_tiles=n_tiles,
                          zero_init=states is None, n_alias=len(alias_args)),
        grid=(batch // nseq, n_tiles),
        in_specs=in_specs + alias_specs,
        out_specs=[pl.BlockSpec((nseq, chunk, GROUP_WIDTH), lambda g, t: (g, t, 0)), hist_spec, h_spec],
        out_shape=[
            jax.ShapeDtypeStruct((batch, seq, GROUP_WIDTH), F32),
            jax.ShapeDtypeStruct((DEPTH, batch, khist, SSM_CONV_DIM), F32),
            jax.ShapeDtypeStruct((DEPTH, batch, n_pairs, 2 * SSM_HEAD_DIM, SSM_STATE), F32),
        ],
        scratch_shapes=[pltpu.VMEM((nseq, CONV_CARRY + chunk, SSM_CONV_DIM), F32)],
        input_output_aliases=aliases,
        compiler_params=pltpu.CompilerParams(dimension_semantics=("parallel", "arbitrary"),
                                             vmem_limit_bytes=VMEM_LIMIT),
        name="ssm_mixer",
    )(*args, *alias_args)
    return out, (hist_new, h_new)


def _conf_kernel(*refs, tile, sub, nseq, n_tiles, zero_init, n_alias):
    a_ref, g_ref = refs[:2]
    n_state = 0 if zero_init else 1
    w_ref, b_ref, lnw_ref, lnb_ref = refs[2 + n_state:6 + n_state]
    o_ref, hist_out_ref, ext_ref = refs[6 + n_state + n_alias:]
    t = pl.program_id(1)
    khist = CONF_KERNEL - 1
    tap0 = CONF_CARRY - khist

    @pl.when(t == 0)
    def _():
        ext_ref[:, 0:CONF_CARRY, :] = jnp.zeros((nseq, CONF_CARRY, GROUP_WIDTH), F32)
        if not zero_init:
            ext_ref[:, tap0:CONF_CARRY, :] = refs[2][...]

    ext_ref[:, CONF_CARRY:CONF_CARRY + tile, :] = a_ref[...] * jax.nn.sigmoid(g_ref[...])

    for s in range(nseq):
        for cb in range(GROUP_WIDTH // LANES):
            col0 = cb * LANES
            for rb in range(tile // sub):
                base = rb * sub
                out = b_ref[:, col0:col0 + LANES]
                for res in range(SUBLANES):
                    nrows = sub if res == 0 else sub + SUBLANES
                    acc = None
                    for j in range(CONF_KERNEL):
                        if (tap0 + j) % SUBLANES != res:
                            continue
                        a0 = base + tap0 + j - res
                        term = ext_ref[s, a0:a0 + nrows, col0:col0 + LANES] * w_ref[j:j + 1, col0:col0 + LANES]
                        acc = term if acc is None else acc + term
                    if acc is not None:
                        out = out + acc[res:res + sub]
                o_ref[s, base:base + sub, col0:col0 + LANES] = out

    cv = o_ref[...]
    mu = jnp.mean(cv, axis=-1, keepdims=True)
    xc = cv - mu
    y = xc * lax.rsqrt(jnp.mean(xc * xc, axis=-1, keepdims=True) + EPS) * lnw_ref[...] + lnb_ref[...]
    o_ref[...] = _silu(y)

    @pl.when(t == n_tiles - 1)
    def _():
        hist_out_ref[...] = ext_ref[:, CONF_CARRY + tile - khist:CONF_CARRY + tile, :]

    if n_tiles > 1:
        ext_ref[:, 0:CONF_CARRY, :] = ext_ref[:, tile:tile + CONF_CARRY, :]


def _conf(proj, hist, prev, dw_w, dw_b, ln_w, ln_b, *, layer, tile, nseq):
    batch, seq, _ = proj.shape
    n_tiles = seq // tile
    khist = CONF_KERNEL - 1
    vec = lambda v: v.reshape(1, GROUP_WIDTH)
    vec_spec = pl.BlockSpec((1, GROUP_WIDTH), lambda g, t: (0, 0))
    hist_spec = _layer_spec((nseq, khist, GROUP_WIDTH), layer, lambda g, t: (g, 0, 0))
    in_specs = [
        pl.BlockSpec((nseq, tile, GROUP_WIDTH), lambda g, t: (g, t, COL_GLU_A // GROUP_WIDTH)),
        pl.BlockSpec((nseq, tile, GROUP_WIDTH), lambda g, t: (g, t, COL_GLU_G // GROUP_WIDTH)),
    ]
    args = [proj, proj]
    if hist is not None:
        in_specs.append(hist_spec)
        args.append(hist)
    in_specs += [pl.BlockSpec((CONF_KERNEL, GROUP_WIDTH), lambda g, t: (0, 0)), vec_spec, vec_spec, vec_spec]
    args += [dw_w, vec(dw_b), vec(ln_w), vec(ln_b)]
    alias_args, alias_specs, aliases = _alias_previous(prev, len(args), 1)
    out, hist_new = pl.pallas_call(
        functools.partial(_conf_kernel, tile=tile, sub=min(tile, 64), nseq=nseq, n_tiles=n_tiles,
                          zero_init=hist is None, n_alias=len(alias_args)),
        grid=(batch // nseq, n_tiles),
        in_specs=in_specs + alias_specs,
        out_specs=[pl.BlockSpec((nseq, tile, GROUP_WIDTH), lambda g, t: (g, t, 0)), hist_spec],
        out_shape=[
            jax.ShapeDtypeStruct((batch, seq, GROUP_WIDTH), F32),
            jax.ShapeDtypeStruct((DEPTH, batch, khist, GROUP_WIDTH), F32),
        ],
        scratch_shapes=[pltpu.VMEM((nseq, CONF_CARRY + tile, GROUP_WIDTH), F32)],
        input_output_aliases=aliases,
        compiler_params=pltpu.CompilerParams(dimension_semantics=("parallel", "arbitrary"),
                                             vmem_limit_bytes=VMEM_LIMIT),
        name="conformer_mixer",
    )(*args, *alias_args)
    return out, (hist_new,)


def _pool_kernel(*refs, tile, nseq, n_tiles, pos0, zero_init, n_alias):
    u_ref = refs[0]
    n_state = 0 if zero_init else 1
    pw_ref, ps_ref = refs[1 + n_state:3 + n_state]
    o_ref, hist_out_ref, ext_ref = refs[3 + n_state + n_alias:]
    t = pl.program_id(1)

    @pl.when(t == 0)
    def _():
        ext_ref[:, 0:POOL_CARRY, :] = jnp.zeros((nseq, POOL_CARRY, GROUP_WIDTH), F32)
        if not zero_init:
            ext_ref[:, POOL_CARRY - POOL_HIST:POOL_CARRY, :] = refs[1][...]

    ext_ref[:, POOL_CARRY:POOL_CARRY + tile, :] = u_ref[...]
    pos = (pos0 + t * tile + lax.broadcasted_iota(jnp.int32, (tile, 1), 0)).astype(F32)

    for s in range(nseq):
        for gi, win in enumerate(POOL_WINDOWS):
            col0 = gi * POOL_GROUP
            acc = None
            for i in range(win):
                term = ext_ref[s, POOL_CARRY - i:POOL_CARRY - i + tile, col0:col0 + POOL_GROUP]
                acc = term if acc is None else acc + term
            cnt = jnp.minimum(jnp.float32(win), pos + 1.0)
            pooled = acc / cnt - u_ref[s, :, col0:col0 + POOL_GROUP]
            y = _dot(pooled, pw_ref[gi])
            o_ref[s, :, col0:col0 + POOL_GROUP] = y * ps_ref[:, col0:col0 + POOL_GROUP]

    @pl.when(t == n_tiles - 1)
    def _():
        hist_out_ref[...] = ext_ref[:, POOL_CARRY + tile - POOL_HIST:POOL_CARRY + tile, :]

    if n_tiles > 1:
        ext_ref[:, 0:POOL_CARRY, :] = ext_ref[:, tile:tile + POOL_CARRY, :]


def _pool(proj, hist, prev, pool_w, pool_scale, *, layer, tile, nseq, pos0):
    batch, seq, _ = proj.shape
    n_tiles = seq // tile
    n_win = len(POOL_WINDOWS)
    hist_spec = _layer_spec((nseq, POOL_HIST, GROUP_WIDTH), layer, lambda g, t: (g, 0, 0))
    in_specs = [pl.BlockSpec((nseq, tile, GROUP_WIDTH), lambda g, t: (g, t, COL_POOL // GROUP_WIDTH))]
    args = [proj]
    if hist is not None:
        in_specs.append(hist_spec)
        args.append(hist)
    in_specs += [pl.BlockSpec((n_win, POOL_GROUP, POOL_GROUP), lambda g, t: (0, 0, 0)),
                 pl.BlockSpec((1, GROUP_WIDTH), lambda g, t: (0, 0))]
    args += [pool_w, pool_scale.reshape(1, GROUP_WIDTH)]
    alias_args, alias_specs, aliases = _alias_previous(prev, len(args), 1)
    out, hist_new = pl.pallas_call(
        functools.partial(_pool_kernel, tile=tile, nseq=nseq, n_tiles=n_tiles, pos0=pos0,
                          zero_init=hist is None, n_alias=len(alias_args)),
        grid=(batch // nseq, n_tiles),
        in_specs=in_specs + alias_specs,
        out_specs=[pl.BlockSpec((nseq, tile, GROUP_WIDTH), lambda g, t: (g, t, 0)), hist_spec],
        out_shape=[
            jax.ShapeDtypeStruct((batch, seq, GROUP_WIDTH), F32),
            jax.ShapeDtypeStruct((DEPTH, batch, POOL_HIST, GROUP_WIDTH), F32),
        ],
        scratch_shapes=[pltpu.VMEM((nseq, POOL_CARRY + tile, GROUP_WIDTH), F32)],
        input_output_aliases=aliases,
        compiler_params=pltpu.CompilerParams(dimension_semantics=("parallel", "arbitrary"),
                                             vmem_limit_bytes=VMEM_LIMIT),
        name="pool_mixer",
    )(*args, *alias_args)
    return out, (hist_new,)


IN_SEGMENTS = ((SRC_QKV, COL_QKV, 3 * GROUP_WIDTH), (SRC_GZ, COL_GZ, GROUP_WIDTH), (SRC_SZ, COL_SZ, GROUP_WIDTH),
               (SRC_POOL, COL_POOL, GROUP_WIDTH), (SRC_GLU, COL_GLU_A, 2 * GROUP_WIDTH),
               (SRC_XBC, COL_X, SSM_CONV_DIM))


def _reorder_w_in_kernel(w_ref, o_ref):
    for src, dst, width in IN_SEGMENTS:
        o_ref[:, dst:dst + width] = w_ref[:, src:src + width].astype(BF16)
    rows = o_ref.shape[0]
    lane = lax.broadcasted_iota(jnp.int32, (rows, LANES), 1)
    dt_tile = (SRC_DT // LANES) * LANES
    gates = w_ref[:, SRC_GB:SRC_GB + LANES]
    dts = w_ref[:, dt_tile:dt_tile + LANES]
    small = jnp.where(lane < LANE_DT, gates, jnp.where(lane < LANE_DT + SSM_HEADS, dts, 0.0))
    o_ref[:, COL_SMALL:COL_SMALL + LANES] = small.astype(BF16)
    o_ref[:, COL_SMALL + LANES:IN_COLS_PAD] = jnp.zeros((rows, IN_COLS_PAD - COL_SMALL - LANES), BF16)


def _reorder_w_in(w_in, *, tk=256):
    depth, d, n = w_in.shape
    assert SRC_GB % LANES == 0 and SRC_GA == SRC_GB + LANE_GA and SRC_DT % LANES == LANE_DT
    return pl.pallas_call(
        _reorder_w_in_kernel,
        grid=(depth, d // tk),
        in_specs=[pl.BlockSpec((None, tk, n), lambda l, i: (l, i, 0))],
        out_specs=pl.BlockSpec((None, tk, IN_COLS_PAD), lambda l, i: (l, i, 0)),
        out_shape=jax.ShapeDtypeStruct((depth, d, IN_COLS_PAD), BF16),
        compiler_params=pltpu.CompilerParams(dimension_semantics=("parallel", "parallel"),
                                             vmem_limit_bytes=VMEM_LIMIT),
        name="reorder_w_in",
    )(w_in)


def kernel(x_prompt, x_sample, state_gdn, state_gdn_conv, state_ssm, state_ssm_conv, state_conv, state_pool,
           norm_mix, w_in, gdn_conv_w, gdn_a_log, gdn_dt_bias, gdn_norm_w,
           ssm_conv_w, ssm_conv_b, ssm_a_log, ssm_dt_bias, ssm_d, ssm_norm_w,
           conf_dw_w, conf_dw_b, conf_ln_w, conf_ln_b, pool_w, pool_scale,
           w_out, norm_ffn, w_gate, w_up, w_down, norm_final):
    nb, seq, d = x_prompt.shape
    db, dseq, _ = x_sample.shape
    rows_p = nb * seq
    rows_s = db * dseq
    n_pairs = SSM_HEADS // 2
    pair_shape = (n_pairs, 2 * SSM_HEAD_DIM, SSM_STATE)

    w_in_b = _reorder_w_in(w_in)
    w_out_b = w_out.astype(BF16)
    w_down_b = w_down.astype(BF16)
    pool_w_b = pool_w.astype(BF16)

    x_in = (x_prompt.reshape(rows_p, d), x_sample.reshape(rows_s, d))
    x = None

    decode_states = dict(
        gdn=(state_gdn_conv, state_gdn),
        ssm=(state_ssm_conv, state_ssm.reshape((DEPTH, db) + pair_shape)),
        conf=state_conv, pool=state_pool)
    no_states = dict(gdn=None, ssm=None, conf=None, pool=None)
    paths = ((0, nb, seq, 64, nb, 128, 1, 0, no_states),
             (rows_p, db, dseq, dseq, 8, dseq, 8, PAST_LEN, decode_states))
    stacked = [dict(gdn=None, ssm=None, conf=None, pool=None) for _ in paths]

    for l in range(DEPTH):
        if l > 0:
            h = _rmsnorm(x, norm_mix[l], BF16)
        for pi, (row0, batch, slen, chunk, rseq, tile, cseq, pos0, st) in enumerate(paths):
            if l == 0:
                proj = _matmul(_rmsnorm(x_in[pi], norm_mix[l], BF16), w_in_b, layer=l, tm=1024, tn=768,
                               name="in_proj")
            else:
                proj = _matmul(h, w_in_b, layer=l, tm=1024, tn=768, name="in_proj", row0=row0,
                               rows=batch * slen)
            proj = proj.reshape(batch, slen, IN_COLS_PAD)
            prev = stacked[pi]
            o_gdn, prev["gdn"] = _gdn(proj, st["gdn"], prev["gdn"], gdn_conv_w[l], gdn_a_log[l],
                                      gdn_dt_bias[l], gdn_norm_w[l], layer=l, chunk=chunk, nseq=rseq)
            o_ssm, prev["ssm"] = _ssm(proj, st["ssm"], prev["ssm"], ssm_conv_w[l], ssm_conv_b[l], ssm_a_log[l],
                                      ssm_dt_bias[l], ssm_d[l], ssm_norm_w[l], layer=l, chunk=chunk, nseq=rseq)
            o_conv, prev["conf"] = _conf(proj, st["conf"], prev["conf"], conf_dw_w[l], conf_dw_b[l],
                                         conf_ln_w[l], conf_ln_b[l], layer=l, tile=tile, nseq=cseq)
            o_pool, prev["pool"] = _pool(proj, st["pool"], prev["pool"], pool_w_b[l], pool_scale[l],
                                         layer=l, tile=tile, nseq=cseq, pos0=pos0)
            parts = [o.reshape(batch * slen, GROUP_WIDTH) for o in (o_gdn, o_ssm, o_conv, o_pool)]
            x = _out_proj(parts, w_out_b, x, layer=l, row0=row0, tm=512, tn=1024,
                          res=x_in[pi] if l == 0 else None, total_rows=rows_p + rows_s)
        h = _rmsnorm(x, norm_ffn[l], BF16)
        ff = _gate_up(h, w_gate, w_up, layer=l, tm=1536, tn=256)
        x = _matmul(ff, w_down_b, x, layer=l, tm=768, tn=256, name="ffn_down")

    y_prompt = _rmsnorm(x, norm_final, F32, row0=0, rows=rows_p).reshape(nb, seq, d)
    y_sample = _rmsnorm(x, norm_final, F32, row0=rows_p, rows=rows_s).reshape(db, dseq, d)

    def states_out(st, batch):
        gdn_conv, gdn_s = st["gdn"]
        ssm_conv, ssm_h = st["ssm"]
        return (gdn_s, gdn_conv, ssm_h.reshape(DEPTH, batch, SSM_HEADS, SSM_HEAD_DIM, SSM_STATE), ssm_conv,
                st["conf"][0], st["pool"][0])

    return (y_prompt, y_sample) + states_out(stacked[0], nb) + states_out(stacked[1], db)
```

```python
import functools

import jax
import jax.numpy as jnp
from jax import lax
from jax.experimental import pallas as pl
from jax.experimental.pallas import tpu as pltpu

F32 = jnp.float32
BF16 = jnp.bfloat16

D_MODEL = 4096
DEPTH = 4
PAST_LEN = 16384
GROUP_WIDTH = 1024
N_MIXERS = 4
GDN_HEADS = 8
GDN_HEAD_DIM = 128
SHORT_CONV = 4
SSM_HEADS = 16
SSM_HEAD_DIM = 64
SSM_GROUPS = 2
SSM_STATE = 128
SSM_CONV_DIM = 1536
CONF_KERNEL = 31
POOL_WINDOWS = (2, 4, 8, 16)
POOL_GROUP = 256
POOL_HIST = 15
D_FF = 11008
EPS = 1e-6

SUBLANES = 8
LANES = 128
BF16_ROWS = 16
MXU_DIM = 256
VMEM_LIMIT = 56 * 1024 * 1024

COL_QKV = 0
COL_GZ = 3072
COL_SZ = 4096
COL_POOL = 5120
COL_GLU_A = 6144
COL_GLU_G = 7168
COL_X = 8192
COL_BC = 9216
COL_SMALL = 9728
LANE_GB = 0
LANE_GA = 8
LANE_DT = 16
IN_COLS_PAD = 9984

SRC_QKV, SRC_GZ, SRC_GB, SRC_GA, SRC_SZ, SRC_XBC, SRC_DT, SRC_GLU, SRC_POOL = (
    0, 3072, 4096, 4104, 4112, 5136, 6672, 6688, 8736)

CONV_CARRY = 8
CONF_CARRY = 32
POOL_CARRY = 16


def _mxu_operands(a, b):
    if a.shape[0] % BF16_ROWS == 0 and b.shape[0] % BF16_ROWS == 0:
        return a.astype(BF16), b.astype(BF16)
    return a.astype(F32), b.astype(F32)


def _dot(a, b):
    a, b = _mxu_operands(a, b)
    return jnp.dot(a, b, preferred_element_type=F32)


def _dot_nt(a, b):
    a, b = _mxu_operands(a, b)
    return lax.dot_general(a, b, (((1,), (1,)), ((), ())), preferred_element_type=F32)


def _dot_tn(a, b):
    a, b = _mxu_operands(a, b)
    return lax.dot_general(a, b, (((0,), (0,)), ((), ())), preferred_element_type=F32)


def _rows_dot_exact(sel, x):
    hi = x.astype(BF16)
    r = x - hi.astype(F32)
    mid = r.astype(BF16)
    lo = (r - mid.astype(F32)).astype(BF16)
    dot = lambda v: jnp.dot(sel, v, preferred_element_type=F32)
    return dot(hi) + (dot(mid) + dot(lo))


def _block_masks(n, shift):
    r = lax.broadcasted_iota(jnp.int32, (n, n), 0)
    q = lax.broadcasted_iota(jnp.int32, (n, n), 1)
    same = (r >> shift) == (q >> shift)
    return same, jnp.logical_and(same, r >= q), jnp.logical_and(same, r > q)


def _silu(x):
    return x * jax.nn.sigmoid(x)


def _softplus(x):
    return jnp.maximum(x, 0.0) + jnp.log1p(jnp.exp(-jnp.abs(x)))


def _layer_spec(block, layer, index_map):
    return pl.BlockSpec((None,) + tuple(block), lambda *idx: (layer,) + tuple(index_map(*idx)))


def _alias_previous(prev, n_inputs, first_output):
    if prev is None:
        return [], [], {}
    specs = [pl.BlockSpec(memory_space=pl.ANY)] * len(prev)
    return list(prev), specs, {n_inputs + k: first_output + k for k in range(len(prev))}


def _rmsnorm_kernel(x_ref, w_ref, o_ref):
    x = x_ref[...]
    ms = jnp.mean(x * x, axis=-1, keepdims=True)
    o_ref[...] = (x * lax.rsqrt(ms + EPS) * w_ref[...]).astype(o_ref.dtype)


def _rmsnorm(x, w, out_dtype, *, row0=0, rows=None, tm=256):
    t, d = x.shape
    rows = t if rows is None else rows
    blk0 = row0 // tm
    return pl.pallas_call(
        _rmsnorm_kernel,
        grid=(rows // tm,),
        in_specs=[pl.BlockSpec((tm, d), lambda i: (blk0 + i, 0)),
                  pl.BlockSpec((1, d), lambda i: (0, 0))],
        out_specs=pl.BlockSpec((tm, d), lambda i: (i, 0)),
        out_shape=jax.ShapeDtypeStruct((rows, d), out_dtype),
        compiler_params=pltpu.CompilerParams(dimension_semantics=("parallel",),
                                             vmem_limit_bytes=VMEM_LIMIT),
        name="rmsnorm",
    )(x, w.reshape(1, d))


def _matmul_kernel(a_ref, b_ref, *rest, has_res):
    part = jnp.dot(a_ref[...], b_ref[...], preferred_element_type=F32)
    if has_res:
        r_ref, o_ref = rest
        o_ref[...] = part + r_ref[...]
    else:
        rest[0][...] = part


def _matmul(a, b, res=None, *, layer, tm, tn, name, row0=0, rows=None):
    kdim = a.shape[1]
    rows = a.shape[0] if rows is None else rows
    n = b.shape[2]
    blk0 = row0 // tm
    in_specs = [pl.BlockSpec((tm, kdim), lambda i, j: (blk0 + i, 0)),
                _layer_spec((kdim, tn), layer, lambda i, j: (0, j))]
    args = [a, b]
    if res is not None:
        in_specs.append(pl.BlockSpec((tm, tn), lambda i, j: (i, j)))
        args.append(res)
    return pl.pallas_call(
        functools.partial(_matmul_kernel, has_res=res is not None),
        grid=(rows // tm, n // tn),
        in_specs=in_specs,
        out_specs=pl.BlockSpec((tm, tn), lambda i, j: (i, j)),
        out_shape=jax.ShapeDtypeStruct((rows, n), F32),
        compiler_params=pltpu.CompilerParams(dimension_semantics=("parallel", "parallel"),
                                             vmem_limit_bytes=VMEM_LIMIT),
        name=name,
    )(*args)


def _out_proj_kernel(*refs):
    part_refs = refs[:N_MIXERS]
    w_ref, x_ref = refs[N_MIXERS:N_MIXERS + 2]
    o_ref, mix_ref = refs[-2:]

    @pl.when(pl.program_id(1) == 0)
    def _():
        for k, p_ref in enumerate(part_refs):
            mix_ref[:, k * GROUP_WIDTH:(k + 1) * GROUP_WIDTH] = p_ref[...].astype(BF16)

    o_ref[...] = jnp.dot(mix_ref[...], w_ref[...], preferred_element_type=F32) + x_ref[...]


def _out_proj(parts, w, x, *, layer, row0, tm, tn, res=None, total_rows=None):
    rows = parts[0].shape[0]
    d = w.shape[2]
    blk0 = row0 // tm
    x_spec = pl.BlockSpec((tm, tn), lambda i, j: (blk0 + i, j))
    in_specs = ([pl.BlockSpec((tm, GROUP_WIDTH), lambda i, j: (i, 0))] * N_MIXERS
                + [_layer_spec((N_MIXERS * GROUP_WIDTH, tn), layer, lambda i, j: (0, j))])
    args = list(parts) + [w]
    if res is None:
        in_specs.append(x_spec)
        args.append(x)
        aliases = {len(args) - 1: 0}
    else:
        in_specs.append(pl.BlockSpec((tm, tn), lambda i, j: (i, j)))
        args.append(res)
        aliases = {}
        if x is not None:
            in_specs.append(pl.BlockSpec(memory_space=pl.ANY))
            args.append(x)
            aliases = {len(args) - 1: 0}
    out_rows = x.shape[0] if x is not None else total_rows
    return pl.pallas_call(
        _out_proj_kernel,
        grid=(rows // tm, d // tn),
        in_specs=in_specs,
        out_specs=x_spec,
        out_shape=jax.ShapeDtypeStruct((out_rows, d), F32),
        scratch_shapes=[pltpu.VMEM((tm, N_MIXERS * GROUP_WIDTH), BF16)],
        input_output_aliases=aliases,
        compiler_params=pltpu.CompilerParams(dimension_semantics=("parallel", "arbitrary"),
                                             vmem_limit_bytes=VMEM_LIMIT),
        name="out_proj",
    )(*args)


def _gate_up_kernel(h_ref, wg_ref, wu_ref, o_ref, wg_bf_ref, wu_bf_ref):
    @pl.when(pl.program_id(1) == 0)
    def _():
        wg_bf_ref[...] = wg_ref[...].astype(BF16)
        wu_bf_ref[...] = wu_ref[...].astype(BF16)

    h = h_ref[...]
    g = jnp.dot(h, wg_bf_ref[...], preferred_element_type=F32)
    u = jnp.dot(h, wu_bf_ref[...], preferred_element_type=F32)
    o_ref[...] = (_silu(g) * u).astype(o_ref.dtype)


def _gate_up(h, wg, wu, *, layer, tm, tn):
    m, d = h.shape
    n = wg.shape[2]
    w_spec = _layer_spec((d, tn), layer, lambda j, i: (0, j))
    return pl.pallas_call(
        _gate_up_kernel,
        grid=(n // tn, m // tm),
        in_specs=[pl.BlockSpec((tm, d), lambda j, i: (i, 0)), w_spec, w_spec],
        out_specs=pl.BlockSpec((tm, tn), lambda j, i: (i, j)),
        out_shape=jax.ShapeDtypeStruct((m, n), BF16),
        scratch_shapes=[pltpu.VMEM((d, tn), BF16), pltpu.VMEM((d, tn), BF16)],
        compiler_params=pltpu.CompilerParams(dimension_semantics=("parallel", "arbitrary"),
                                             vmem_limit_bytes=VMEM_LIMIT),
        name="ffn_gate_up",
    )(h, wg, wu)


def _gdn_kernel(*refs, chunk, nseq, n_tiles, hpg, zero_init, n_alias):
    qkv_ref, z_ref, sm_ref = refs[:3]
    n_state = 0 if zero_init else 2
    cw_ref, gv_ref, nw_ref = refs[3 + n_state:6 + n_state]
    o_ref, hist_out_ref, s_ref, ext_ref = refs[6 + n_state + n_alias:]
    c = chunk
    rows = nseq * c
    gr = hpg * rows
    shift = c.bit_length() - 1
    t = pl.program_id(1)

    @pl.when(t == 0)
    def _():
        ext_ref[:, 0:CONV_CARRY, :] = jnp.zeros((nseq, CONV_CARRY, ext_ref.shape[2]), F32)
        if zero_init:
            s_ref[...] = jnp.zeros(s_ref.shape, F32)
        else:
            ext_ref[:, CONV_CARRY - (SHORT_CONV - 1):CONV_CARRY, :] = refs[3][...]
            s_ref[...] = refs[4][...]

    ext_ref[:, CONV_CARRY:CONV_CARRY + c, :] = qkv_ref[...]

    same_t, tril_t, _ = _block_masks(rows, shift)
    _, tril_g, strict_g = _block_masks(gr, shift)

    sm = sm_ref[...].reshape(rows, LANES)
    beta_arr = jax.nn.sigmoid(sm)
    g_arr = -jnp.exp(gv_ref[0:1, :]) * _softplus(sm + gv_ref[1:2, :])
    gc_arr = _rows_dot_exact(jnp.where(tril_t, 1.0, 0.0).astype(BF16), g_arr)
    gl_arr = _rows_dot_exact(jnp.where(same_t, 1.0, 0.0).astype(BF16), g_arr)

    def conv(s, col0):
        acc = None
        for j in range(SHORT_CONV):
            r0 = CONV_CARRY - (SHORT_CONV - 1) + j
            term = ext_ref[s, r0:r0 + c, col0:col0 + LANES] * cw_ref[j:j + 1, col0:col0 + LANES]
            acc = term if acc is None else acc + term
        return acc

    for grp in range(GDN_HEADS // hpg):
        heads = range(grp * hpg, (grp + 1) * hpg)
        pairs = [(h, s) for h in heads for s in range(nseq)]
        stack = lambda fn: jnp.concatenate([fn(h, s) for h, s in pairs], axis=0)
        column = lambda arr, lane0: jnp.concatenate(
            [arr[:, lane0 + h:lane0 + h + 1] for h in heads], axis=0)

        qh = _silu(stack(lambda h, s: conv(s, h * GDN_HEAD_DIM)))
        kh = _silu(stack(lambda h, s: conv(s, GROUP_WIDTH + h * GDN_HEAD_DIM)))
        vh = _silu(stack(lambda h, s: conv(s, 2 * GROUP_WIDTH + h * GDN_HEAD_DIM)))
        zs = stack(lambda h, s: z_ref[s, :, h * GDN_HEAD_DIM:(h + 1) * GDN_HEAD_DIM])
        qn = qh * lax.rsqrt(jnp.sum(qh * qh, axis=-1, keepdims=True) + EPS) * (GDN_HEAD_DIM ** -0.5)
        kn = kh * lax.rsqrt(jnp.sum(kh * kh, axis=-1, keepdims=True) + EPS)
        beta = column(beta_arr, LANE_GB)
        gcol = column(gc_arr, LANE_GA)
        glast = column(gl_arr, LANE_GA)
        grow = jnp.broadcast_to(gcol, (gr, LANES)).T[0:1, :]
        decay = jnp.where(tril_g, jnp.exp(jnp.where(tril_g, gcol - grow, 0.0)), 0.0)
        kb = kn * beta
        vb = vh * beta
        lm = jnp.where(strict_g, _dot_nt(kb, kn) * decay, 0.0)

        r = -lm
        p = r
        for _ in range(shift - 1):
            p = _dot(p, p)
            r = r + p + _dot(r, p)

        egc = jnp.exp(gcol)
        rhs = jnp.concatenate([vb, kb * egc], axis=1)
        sol = rhs + _dot(r, rhs)
        u = sol[:, :GDN_HEAD_DIM]
        w = sol[:, GDN_HEAD_DIM:]
        attn = jnp.where(tril_g, _dot_nt(qn, kn) * decay, 0.0)
        qe = qn * egc
        kdec = kn * jnp.exp(glast - gcol)

        v_parts, o_parts = [], []
        for i, (h, s) in enumerate(pairs):
            st = s_ref[s, h]
            v_parts.append(u[i * c:(i + 1) * c] - _dot(w[i * c:(i + 1) * c], st))
            o_parts.append(_dot(qe[i * c:(i + 1) * c], st))
        v_new = jnp.concatenate(v_parts, axis=0)
        o = jnp.concatenate(o_parts, axis=0) + _dot(attn, v_new)
        for i, (h, s) in enumerate(pairs):
            gl = gl_arr[s * c:s * c + 1, LANE_GA + h:LANE_GA + h + 1]
            s_ref[s, h] = (s_ref[s, h] * jnp.exp(gl)
                           + _dot_tn(kdec[i * c:(i + 1) * c], v_new[i * c:(i + 1) * c]))

        on = o * lax.rsqrt(jnp.mean(o * o, axis=-1, keepdims=True) + EPS) * nw_ref[...]
        res = on * _silu(zs)
        for i, (h, s) in enumerate(pairs):
            o_ref[s, :, h * GDN_HEAD_DIM:(h + 1) * GDN_HEAD_DIM] = res[i * c:(i + 1) * c].astype(o_ref.dtype)

    @pl.when(t == n_tiles - 1)
    def _():
        hist_out_ref[...] = ext_ref[:, CONV_CARRY + c - (SHORT_CONV - 1):CONV_CARRY + c, :]

    if n_tiles > 1:
        ext_ref[:, 0:CONV_CARRY, :] = ext_ref[:, c:c + CONV_CARRY, :]


def _gdn(proj, states, prev, conv_w, a_log, dt_bias, norm_w, *, layer, chunk, nseq, out_dtype):
    batch, seq, _ = proj.shape
    n_tiles = seq // chunk
    hpg = MXU_DIM // (nseq * chunk)
    w3 = 3 * GROUP_WIDTH
    khist = SHORT_CONV - 1
    gv = jnp.zeros((SUBLANES, LANES), F32)
    gv = gv.at[0, LANE_GA:LANE_GA + GDN_HEADS].set(a_log).at[1, LANE_GA:LANE_GA + GDN_HEADS].set(dt_bias)
    hist_spec = _layer_spec((nseq, khist, w3), layer, lambda g, t: (g, 0, 0))
    s_spec = _layer_spec((nseq, GDN_HEADS, GDN_HEAD_DIM, GDN_HEAD_DIM), layer, lambda g, t: (g, 0, 0, 0))
    in_specs = [
        pl.BlockSpec((nseq, chunk, w3), lambda g, t: (g, t, COL_QKV // w3)),
        pl.BlockSpec((nseq, chunk, GROUP_WIDTH), lambda g, t: (g, t, COL_GZ // GROUP_WIDTH)),
        pl.BlockSpec((nseq, chunk, LANES), lambda g, t: (g, t, COL_SMALL // LANES)),
    ]
    args = [proj, proj, proj]
    if states is not None:
        in_specs += [hist_spec, s_spec]
        args += list(states)
    in_specs += [pl.BlockSpec((SHORT_CONV, w3), lambda g, t: (0, 0)),
                 pl.BlockSpec((SUBLANES, LANES), lambda g, t: (0, 0)),
                 pl.BlockSpec((1, GDN_HEAD_DIM), lambda g, t: (0, 0))]
    args += [conv_w, gv, norm_w.reshape(1, GDN_HEAD_DIM)]
    alias_args, alias_specs, aliases = _alias_previous(prev, len(args), 1)
    out, hist_new, s_new = pl.pallas_call(
        functools.partial(_gdn_kernel, chunk=chunk, nseq=nseq, n_tiles=n_tiles, hpg=hpg,
                          zero_init=states is None, n_alias=len(alias_args)),
        grid=(batch // nseq, n_tiles),
        in_specs=in_specs + alias_specs,
        out_specs=[pl.BlockSpec((nseq, chunk, GROUP_WIDTH), lambda g, t: (g, t, 0)), hist_spec, s_spec],
        out_shape=[
            jax.ShapeDtypeStruct((batch, seq, GROUP_WIDTH), out_dtype),
            jax.ShapeDtypeStruct((DEPTH, batch, khist, w3), F32),
            jax.ShapeDtypeStruct((DEPTH, batch, GDN_HEADS, GDN_HEAD_DIM, GDN_HEAD_DIM), F32),
        ],
        scratch_shapes=[pltpu.VMEM((nseq, CONV_CARRY + chunk, w3), F32)],
        input_output_aliases=aliases,
        compiler_params=pltpu.CompilerParams(dimension_semantics=("parallel", "arbitrary"),
                                             vmem_limit_bytes=VMEM_LIMIT),
        name="gdn_mixer",
    )(*args, *alias_args)
    return out, (hist_new, s_new)


def _ssm_kernel(*refs, chunk, nseq, n_tiles, zero_init, n_alias):
    z_ref, x_ref, bc_ref, sm_ref = refs[:4]
    n_state = 0 if zero_init else 2
    cw_ref, cb_ref, sv_ref, dskip_ref, nw_ref = refs[4 + n_state:9 + n_state]
    o_ref, hist_out_ref, h_ref, ext_ref = refs[9 + n_state + n_alias:]
    c = chunk
    t = pl.program_id(1)
    n_pairs = SSM_HEADS // 2
    pairs_per_group = n_pairs // SSM_GROUPS
    x_blocks = GROUP_WIDTH // LANES
    group_w = GROUP_WIDTH // SSM_GROUPS

    @pl.when(t == 0)
    def _():
        ext_ref[:, 0:CONV_CARRY, :] = jnp.zeros((nseq, CONV_CARRY, SSM_CONV_DIM), F32)
        if zero_init:
            h_ref[...] = jnp.zeros(h_ref.shape, F32)
        else:
            ext_ref[:, CONV_CARRY - (SHORT_CONV - 1):CONV_CARRY, :] = refs[4][...]
            h_ref[...] = refs[5][...]

    ext_ref[:, CONV_CARRY:CONV_CARRY + c, 0:GROUP_WIDTH] = x_ref[...]
    ext_ref[:, CONV_CARRY:CONV_CARRY + c, GROUP_WIDTH:SSM_CONV_DIM] = bc_ref[...]

    rows = lax.broadcasted_iota(jnp.int32, (c, c), 0)
    cols = lax.broadcasted_iota(jnp.int32, (c, c), 1)
    causal = rows >= cols
    tril_b = jnp.where(causal, 1.0, 0.0).astype(BF16)
    lane_lo = lax.broadcasted_iota(jnp.int32, (c, LANES), 1) < SSM_HEAD_DIM
    row_lo = lax.broadcasted_iota(jnp.int32, (2 * SSM_HEAD_DIM, SSM_STATE), 0) < SSM_HEAD_DIM
    neg_a = -jnp.exp(sv_ref[0:1, :])

    for s in range(nseq):
        sm = sm_ref[s]
        dt_arr = _softplus(sm + sv_ref[1:2, :])
        acum = _rows_dot_exact(tril_b, dt_arr * neg_a)
        acum_t = acum.T
        dt_t = dt_arr.T

        def conv_act(blk):
            col0 = blk * LANES
            acc = cb_ref[:, col0:col0 + LANES]
            for j in range(SHORT_CONV):
                r0 = CONV_CARRY - (SHORT_CONV - 1) + j
                acc = acc + ext_ref[s, r0:r0 + c, col0:col0 + LANES] * cw_ref[j:j + 1, col0:col0 + LANES]
            return _silu(acc)

        bm = [conv_act(x_blocks + g) for g in range(SSM_GROUPS)]
        cm = [conv_act(x_blocks + SSM_GROUPS + g) for g in range(SSM_GROUPS)]
        cbm = [_dot_nt(cm[g], bm[g]) for g in range(SSM_GROUPS)]

        ys = []
        for p in range(n_pairs):
            g = p // pairs_per_group
            xp = conv_act(p)
            yd, cs, ecol, cdec = [], [], [], []
            for hh in range(2):
                lane = LANE_DT + 2 * p + hh
                col = acum[:, lane:lane + 1]
                row = acum_t[lane:lane + 1, :]
                dtrow = dt_t[lane:lane + 1, :]
                dtcol = dt_arr[:, lane:lane + 1]
                alast = acum[c - 1:c, lane:lane + 1]
                lmat = jnp.where(causal, jnp.exp(jnp.where(causal, col - row, 0.0)), 0.0)
                yd.append(_dot(cbm[g] * lmat * dtrow, xp))
                cs.append(_dot_tn(xp * (jnp.exp(alast - col) * dtcol), bm[g]))
                ecol.append(jnp.exp(col))
                cdec.append(jnp.exp(alast))
            hp = h_ref[s, p]
            y = jnp.where(lane_lo, yd[0], yd[1])
            y = y + _dot_nt(cm[g], hp) * jnp.where(lane_lo, ecol[0], ecol[1])
            h_ref[s, p] = hp * jnp.where(row_lo, cdec[0], cdec[1]) + jnp.where(row_lo, cs[0], cs[1])
            y = y + dskip_ref[:, p * LANES:(p + 1) * LANES] * xp
            ys.append(y * _silu(z_ref[s, :, p * LANES:(p + 1) * LANES]))

        for g in range(SSM_GROUPS):
            blocks = ys[g * pairs_per_group:(g + 1) * pairs_per_group]
            ssq = None
            for y in blocks:
                sq = jnp.sum(y * y, axis=-1, keepdims=True)
                ssq = sq if ssq is None else ssq + sq
            scale = lax.rsqrt(ssq / group_w + EPS)
            for i, y in enumerate(blocks):
                col0 = (g * pairs_per_group + i) * LANES
                o_ref[s, :, col0:col0 + LANES] = (y * scale * nw_ref[:, col0:col0 + LANES]).astype(o_ref.dtype)

    @pl.when(t == n_tiles - 1)
    def _():
        hist_out_ref[...] = ext_ref[:, CONV_CARRY + c - (SHORT_CONV - 1):CONV_CARRY + c, :]

    if n_tiles > 1:
        ext_ref[:, 0:CONV_CARRY, :] = ext_ref[:, c:c + CONV_CARRY, :]


def _ssm(proj, states, prev, conv_w, conv_b, a_log, dt_bias, d_skip, norm_w, *, layer, chunk, nseq, out_dtype):
    batch, seq, _ = proj.shape
    n_tiles = seq // chunk
    n_pairs = SSM_HEADS // 2
    bc_w = SSM_CONV_DIM - GROUP_WIDTH
    khist = SHORT_CONV - 1
    sv = jnp.zeros((SUBLANES, LANES), F32)
    sv = sv.at[0, LANE_DT:LANE_DT + SSM_HEADS].set(a_log).at[1, LANE_DT:LANE_DT + SSM_HEADS].set(dt_bias)
    hist_spec = _layer_spec((nseq, khist, SSM_CONV_DIM), layer, lambda g, t: (g, 0, 0))
    h_spec = _layer_spec((nseq, n_pairs, 2 * SSM_HEAD_DIM, SSM_STATE), layer, lambda g, t: (g, 0, 0, 0))
    vec_spec = lambda width: pl.BlockSpec((1, width), lambda g, t: (0, 0))
    in_specs = [
        pl.BlockSpec((nseq, chunk, GROUP_WIDTH), lambda g, t: (g, t, COL_SZ // GROUP_WIDTH)),
        pl.BlockSpec((nseq, chunk, GROUP_WIDTH), lambda g, t: (g, t, COL_X // GROUP_WIDTH)),
        pl.BlockSpec((nseq, chunk, bc_w), lambda g, t: (g, t, COL_BC // bc_w)),
        pl.BlockSpec((nseq, chunk, LANES), lambda g, t: (g, t, COL_SMALL // LANES)),
    ]
    args = [proj, proj, proj, proj]
    if states is not None:
        in_specs += [hist_spec, h_spec]
        args += list(states)
    in_specs += [pl.BlockSpec((SHORT_CONV, SSM_CONV_DIM), lambda g, t: (0, 0)), vec_spec(SSM_CONV_DIM),
                 pl.BlockSpec((SUBLANES, LANES), lambda g, t: (0, 0)), vec_spec(GROUP_WIDTH),
                 vec_spec(GROUP_WIDTH)]
    args += [conv_w, conv_b.reshape(1, SSM_CONV_DIM), sv,
             jnp.repeat(d_skip, SSM_HEAD_DIM).reshape(1, GROUP_WIDTH), norm_w.reshape(1, GROUP_WIDTH)]
    alias_args, alias_specs, aliases = _alias_previous(prev, len(args), 1)
    out, hist_new, h_new = pl.pallas_call(
        functools.partial(_ssm_kernel, chunk=chunk, nseq=nseq, n_tiles=n_tiles,
                          zero_init=states is None, n_alias=len(alias_args)),
        grid=(batch // nseq, n_tiles),
        in_specs=in_specs + alias_specs,
        out_specs=[pl.BlockSpec((nseq, chunk, GROUP_WIDTH), lambda g, t: (g, t, 0)), hist_spec, h_spec],
        out_shape=[
            jax.ShapeDtypeStruct((batch, seq, GROUP_WIDTH), out_dtype),
            jax.ShapeDtypeStruct((DEPTH, batch, khist, SSM_CONV_DIM), F32),
            jax.ShapeDtypeStruct((DEPTH, batch, n_pairs, 2 * SSM_HEAD_DIM, SSM_STATE), F32),
        ],
        scratch_shapes=[pltpu.VMEM((nseq, CONV_CARRY + chunk, SSM_CONV_DIM), F32)],
        input_output_aliases=aliases,
        compiler_params=pltpu.CompilerParams(dimension_semantics=("parallel", "arbitrary"),
                                             vmem_limit_bytes=VMEM_LIMIT),
        name="ssm_mixer",
    )(*args, *alias_args)
    return out, (hist_new, h_new)


def _conf_kernel(*refs, tile, sub, nseq, n_tiles, zero_init, n_alias):
    a_ref, g_ref = refs[:2]
    n_state = 0 if zero_init else 1
    w_ref, b_ref, lnw_ref, lnb_ref = refs[2 + n_state:6 + n_state]
    o_ref, hist_out_ref, ext_ref, cv_ref = refs[6 + n_state + n_alias:]
    t = pl.program_id(1)
    khist = CONF_KERNEL - 1
    tap0 = CONF_CARRY - khist

    @pl.when(t == 0)
    def _():
        ext_ref[:, 0:CONF_CARRY, :] = jnp.zeros((nseq, CONF_CARRY, GROUP_WIDTH), F32)
        if not zero_init:
            ext_ref[:, tap0:CONF_CARRY, :] = refs[2][...]

    ext_ref[:, CONF_CARRY:CONF_CARRY + tile, :] = a_ref[...] * jax.nn.sigmoid(g_ref[...])

    for s in range(nseq):
        for cb in range(GROUP_WIDTH // LANES):
            col0 = cb * LANES
            for rb in range(tile // sub):
                base = rb * sub
                out = b_ref[:, col0:col0 + LANES]
                for res in range(SUBLANES):
                    nrows = sub if res == 0 else sub + SUBLANES
                    acc = None
                    for j in range(CONF_KERNEL):
                        if (tap0 + j) % SUBLANES != res:
                            continue
                        a0 = base + tap0 + j - res
                        term = ext_ref[s, a0:a0 + nrows, col0:col0 + LANES] * w_ref[j:j + 1, col0:col0 + LANES]
                        acc = term if acc is None else acc + term
                    if acc is not None:
                        out = out + acc[res:res + sub]
                cv_ref[s, base:base + sub, col0:col0 + LANES] = out

    cv = cv_ref[...]
    mu = jnp.mean(cv, axis=-1, keepdims=True)
    xc = cv - mu
    y = xc * lax.rsqrt(jnp.mean(xc * xc, axis=-1, keepdims=True) + EPS) * lnw_ref[...] + lnb_ref[...]
    o_ref[...] = _silu(y).astype(o_ref.dtype)

    @pl.when(t == n_tiles - 1)
    def _():
        hist_out_ref[...] = ext_ref[:, CONF_CARRY + tile - khist:CONF_CARRY + tile, :]

    if n_tiles > 1:
        ext_ref[:, 0:CONF_CARRY, :] = ext_ref[:, tile:tile + CONF_CARRY, :]


def _conf(proj, hist, prev, dw_w, dw_b, ln_w, ln_b, *, layer, tile, nseq, out_dtype):
    batch, seq, _ = proj.shape
    n_tiles = seq // tile
    khist = CONF_KERNEL - 1
    vec = lambda v: v.reshape(1, GROUP_WIDTH)
    vec_spec = pl.BlockSpec((1, GROUP_WIDTH), lambda g, t: (0, 0))
    hist_spec = _layer_spec((nseq, khist, GROUP_WIDTH), layer, lambda g, t: (g, 0, 0))
    in_specs = [
        pl.BlockSpec((nseq, tile, GROUP_WIDTH), lambda g, t: (g, t, COL_GLU_A // GROUP_WIDTH)),
        pl.BlockSpec((nseq, tile, GROUP_WIDTH), lambda g, t: (g, t, COL_GLU_G // GROUP_WIDTH)),
    ]
    args = [proj, proj]
    if hist is not None:
        in_specs.append(hist_spec)
        args.append(hist)
    in_specs += [pl.BlockSpec((CONF_KERNEL, GROUP_WIDTH), lambda g, t: (0, 0)), vec_spec, vec_spec, vec_spec]
    args += [dw_w, vec(dw_b), vec(ln_w), vec(ln_b)]
    alias_args, alias_specs, aliases = _alias_previous(prev, len(args), 1)
    out, hist_new = pl.pallas_call(
        functools.partial(_conf_kernel, tile=tile, sub=min(tile, 64), nseq=nseq, n_tiles=n_tiles,
                          zero_init=hist is None, n_alias=len(alias_args)),
        grid=(batch // nseq, n_tiles),
        in_specs=in_specs + alias_specs,
        out_specs=[pl.BlockSpec((nseq, tile, GROUP_WIDTH), lambda g, t: (g, t, 0)), hist_spec],
        out_shape=[
            jax.ShapeDtypeStruct((batch, seq, GROUP_WIDTH), out_dtype),
            jax.ShapeDtypeStruct((DEPTH, batch, khist, GROUP_WIDTH), F32),
        ],
        scratch_shapes=[pltpu.VMEM((nseq, CONF_CARRY + tile, GROUP_WIDTH), F32),
                        pltpu.VMEM((nseq, tile, GROUP_WIDTH), F32)],
        input_output_aliases=aliases,
        compiler_params=pltpu.CompilerParams(dimension_semantics=("parallel", "arbitrary"),
                                             vmem_limit_bytes=VMEM_LIMIT),
        name="conformer_mixer",
    )(*args, *alias_args)
    return out, (hist_new,)


def _pool_kernel(*refs, tile, nseq, n_tiles, pos0, zero_init, n_alias):
    u_ref = refs[0]
    n_state = 0 if zero_init else 1
    pw_ref, ps_ref = refs[1 + n_state:3 + n_state]
    o_ref, hist_out_ref, ext_ref = refs[3 + n_state + n_alias:]
    t = pl.program_id(1)

    @pl.when(t == 0)
    def _():
        ext_ref[:, 0:POOL_CARRY, :] = jnp.zeros((nseq, POOL_CARRY, GROUP_WIDTH), F32)
        if not zero_init:
            ext_ref[:, POOL_CARRY - POOL_HIST:POOL_CARRY, :] = refs[1][...]

    ext_ref[:, POOL_CARRY:POOL_CARRY + tile, :] = u_ref[...]
    pos = (pos0 + t * tile + lax.broadcasted_iota(jnp.int32, (tile, 1), 0)).astype(F32)

    for s in range(nseq):
        for gi, win in enumerate(POOL_WINDOWS):
            col0 = gi * POOL_GROUP
            acc = None
            for i in range(win):
                term = ext_ref[s, POOL_CARRY - i:POOL_CARRY - i + tile, col0:col0 + POOL_GROUP]
                acc = term if acc is None else acc + term
            cnt = jnp.minimum(jnp.float32(win), pos + 1.0)
            pooled = acc / cnt - u_ref[s, :, col0:col0 + POOL_GROUP]
            y = _dot(pooled, pw_ref[gi])
            o_ref[s, :, col0:col0 + POOL_GROUP] = (y * ps_ref[:, col0:col0 + POOL_GROUP]).astype(o_ref.dtype)

    @pl.when(t == n_tiles - 1)
    def _():
        hist_out_ref[...] = ext_ref[:, POOL_CARRY + tile - POOL_HIST:POOL_CARRY + tile, :]

    if n_tiles > 1:
        ext_ref[:, 0:POOL_CARRY, :] = ext_ref[:, tile:tile + POOL_CARRY, :]


def _pool(proj, hist, prev, pool_w, pool_scale, *, layer, tile, nseq, pos0, out_dtype):
    batch, seq, _ = proj.shape
    n_tiles = seq // tile
    n_win = len(POOL_WINDOWS)
    hist_spec = _layer_spec((nseq, POOL_HIST, GROUP_WIDTH), layer, lambda g, t: (g, 0, 0))
    in_specs = [pl.BlockSpec((nseq, tile, GROUP_WIDTH), lambda g, t: (g, t, COL_POOL // GROUP_WIDTH))]
    args = [proj]
    if hist is not None:
        in_specs.append(hist_spec)
        args.append(hist)
    in_specs += [pl.BlockSpec((n_win, POOL_GROUP, POOL_GROUP), lambda g, t: (0, 0, 0)),
                 pl.BlockSpec((1, GROUP_WIDTH), lambda g, t: (0, 0))]
    args += [pool_w, pool_scale.reshape(1, GROUP_WIDTH)]
    alias_args, alias_specs, aliases = _alias_previous(prev, len(args), 1)
    out, hist_new = pl.pallas_call(
        functools.partial(_pool_kernel, tile=tile, nseq=nseq, n_tiles=n_tiles, pos0=pos0,
                          zero_init=hist is None, n_alias=len(alias_args)),
        grid=(batch // nseq, n_tiles),
        in_specs=in_specs + alias_specs,
        out_specs=[pl.BlockSpec((nseq, tile, GROUP_WIDTH), lambda g, t: (g, t, 0)), hist_spec],
        out_shape=[
            jax.ShapeDtypeStruct((batch, seq, GROUP_WIDTH), out_dtype),
            jax.ShapeDtypeStruct((DEPTH, batch, POOL_HIST, GROUP_WIDTH), F32),
        ],
        scratch_shapes=[pltpu.VMEM((nseq, POOL_CARRY + tile, GROUP_WIDTH), F32)],
        input_output_aliases=aliases,
        compiler_params=pltpu.CompilerParams(dimension_semantics=("parallel", "arbitrary"),
                                             vmem_limit_bytes=VMEM_LIMIT),
        name="pool_mixer",
    )(*args, *alias_args)
    return out, (hist_new,)


IN_SEGMENTS = ((SRC_QKV, COL_QKV, 3 * GROUP_WIDTH), (SRC_GZ, COL_GZ, GROUP_WIDTH), (SRC_SZ, COL_SZ, GROUP_WIDTH),
               (SRC_POOL, COL_POOL, GROUP_WIDTH), (SRC_GLU, COL_GLU_A, 2 * GROUP_WIDTH),
               (SRC_XBC, COL_X, SSM_CONV_DIM))


def _reorder_w_in_kernel(w_ref, o_ref):
    for src, dst, width in IN_SEGMENTS:
        o_ref[:, dst:dst + width] = w_ref[:, src:src + width].astype(BF16)
    rows = o_ref.shape[0]
    lane = lax.broadcasted_iota(jnp.int32, (rows, LANES), 1)
    dt_tile = (SRC_DT // LANES) * LANES
    gates = w_ref[:, SRC_GB:SRC_GB + LANES]
    dts = w_ref[:, dt_tile:dt_tile + LANES]
    small = jnp.where(lane < LANE_DT, gates, jnp.where(lane < LANE_DT + SSM_HEADS, dts, 0.0))
    o_ref[:, COL_SMALL:COL_SMALL + LANES] = small.astype(BF16)
    o_ref[:, COL_SMALL + LANES:IN_COLS_PAD] = jnp.zeros((rows, IN_COLS_PAD - COL_SMALL - LANES), BF16)


def _reorder_w_in(w_in, *, tk=256):
    depth, d, n = w_in.shape
    assert SRC_GB % LANES == 0 and SRC_GA == SRC_GB + LANE_GA and SRC_DT % LANES == LANE_DT
    return pl.pallas_call(
        _reorder_w_in_kernel,
        grid=(depth, d // tk),
        in_specs=[pl.BlockSpec((None, tk, n), lambda l, i: (l, i, 0))],
        out_specs=pl.BlockSpec((None, tk, IN_COLS_PAD), lambda l, i: (l, i, 0)),
        out_shape=jax.ShapeDtypeStruct((depth, d, IN_COLS_PAD), BF16),
        compiler_params=pltpu.CompilerParams(dimension_semantics=("parallel", "parallel"),
                                             vmem_limit_bytes=VMEM_LIMIT),
        name="reorder_w_in",
    )(w_in)


def kernel(x_prompt, x_sample, state_gdn, state_gdn_conv, state_ssm, state_ssm_conv, state_conv, state_pool,
           norm_mix, w_in, gdn_conv_w, gdn_a_log, gdn_dt_bias, gdn_norm_w,
           ssm_conv_w, ssm_conv_b, ssm_a_log, ssm_dt_bias, ssm_d, ssm_norm_w,
           conf_dw_w, conf_dw_b, conf_ln_w, conf_ln_b, pool_w, pool_scale,
           w_out, norm_ffn, w_gate, w_up, w_down, norm_final):
    nb, seq, d = x_prompt.shape
    db, dseq, _ = x_sample.shape
    rows_p = nb * seq
    rows_s = db * dseq
    n_pairs = SSM_HEADS // 2
    pair_shape = (n_pairs, 2 * SSM_HEAD_DIM, SSM_STATE)

    w_in_b = _reorder_w_in(w_in)
    w_out_b = w_out.astype(BF16)
    w_down_b = w_down.astype(BF16)
    pool_w_b = pool_w.astype(BF16)

    x_in = (x_prompt.reshape(rows_p, d), x_sample.reshape(rows_s, d))
    x = None

    decode_states = dict(
        gdn=(state_gdn_conv, state_gdn),
        ssm=(state_ssm_conv, state_ssm.reshape((DEPTH, db) + pair_shape)),
        conf=state_conv, pool=state_pool)
    no_states = dict(gdn=None, ssm=None, conf=None, pool=None)
    paths = ((0, nb, seq, 64, nb, 128, 1, 0, no_states),
             (rows_p, db, dseq, dseq, 8, dseq, 8, PAST_LEN, decode_states))
    stacked = [dict(gdn=None, ssm=None, conf=None, pool=None) for _ in paths]

    for l in range(DEPTH):
        if l > 0:
            h = _rmsnorm(x, norm_mix[l], BF16)
        for pi, (row0, batch, slen, chunk, rseq, tile, cseq, pos0, st) in enumerate(paths):
            if l == 0:
                proj = _matmul(_rmsnorm(x_in[pi], norm_mix[l], BF16), w_in_b, layer=l, tm=1024, tn=768,
                               name="in_proj")
            else:
                proj = _matmul(h, w_in_b, layer=l, tm=1024, tn=768, name="in_proj", row0=row0,
                               rows=batch * slen)
            proj = proj.reshape(batch, slen, IN_COLS_PAD)
            prev = stacked[pi]
            mix_dtype = BF16 if min(chunk, tile) % BF16_ROWS == 0 else F32
            o_gdn, prev["gdn"] = _gdn(proj, st["gdn"], prev["gdn"], gdn_conv_w[l], gdn_a_log[l], gdn_dt_bias[l],
                                      gdn_norm_w[l], layer=l, chunk=chunk, nseq=rseq, out_dtype=mix_dtype)
            o_ssm, prev["ssm"] = _ssm(proj, st["ssm"], prev["ssm"], ssm_conv_w[l], ssm_conv_b[l], ssm_a_log[l],
                                      ssm_dt_bias[l], ssm_d[l], ssm_norm_w[l], layer=l, chunk=chunk, nseq=rseq,
                                      out_dtype=mix_dtype)
            o_conv, prev["conf"] = _conf(proj, st["conf"], prev["conf"], conf_dw_w[l], conf_dw_b[l],
                                         conf_ln_w[l], conf_ln_b[l], layer=l, tile=tile, nseq=cseq,
                                         out_dtype=mix_dtype)
            o_pool, prev["pool"] = _pool(proj, st["pool"], prev["pool"], pool_w_b[l], pool_scale[l],
                                         layer=l, tile=tile, nseq=cseq, pos0=pos0, out_dtype=mix_dtype)
            parts = [o.reshape(batch * slen, GROUP_WIDTH) for o in (o_gdn, o_ssm, o_conv, o_pool)]
            tm, tn = (1024, 512) if mix_dtype == BF16 else (512, 1024)
            x = _out_proj(parts, w_out_b, x, layer=l, row0=row0, tm=tm, tn=tn,
                          res=x_in[pi] if l == 0 else None, total_rows=rows_p + rows_s)
        h = _rmsnorm(x, norm_ffn[l], BF16)
        ff = _gate_up(h, w_gate, w_up, layer=l, tm=1536, tn=256)
        x = _matmul(ff, w_down_b, x, layer=l, tm=768, tn=256, name="ffn_down")

    y_prompt = _rmsnorm(x, norm_final, F32, row0=0, rows=rows_p).reshape(nb, seq, d)
    y_sample = _rmsnorm(x, norm_final, F32, row0=rows_p, rows=rows_s).reshape(db, dseq, d)

    def states_out(st, batch):
        gdn_conv, gdn_s = st["gdn"]
        ssm_conv, ssm_h = st["ssm"]
        return (gdn_s, gdn_conv, ssm_h.reshape(DEPTH, batch, SSM_HEADS, SSM_HEAD_DIM, SSM_STATE), ssm_conv,
                st["conf"][0], st["pool"][0])

    return (y_prompt, y_sample) + states_out(stacked[0], nb) + states_out(stacked[1], db)
```
